```python
import functools
import jax, jax.numpy as jnp
from jax import lax
import numpy as np

D_MODEL = 2048
BATCH = 8
SEQ = 2048
DEPTH = 1
DEC_BATCH = 128
DEC_SEQ = 8
PAST_LEN = 16384
PAGE_SIZE = 128

N_HEADS = 8
Q_LORA = 512
KV_LORA = 512
NOPE_DIM = 128
ROPE_DIM = 64
V_DIM = 128
ROPE_BASE = 10000.0
D_ATTN = N_HEADS * V_DIM
SCALE = (NOPE_DIM + ROPE_DIM) ** -0.5
Q_BLOCK = 128
D_CONV = D_MODEL - D_ATTN
CONV_W = 3
D_IN = Q_LORA + KV_LORA + ROPE_DIM + 3 * D_CONV
D_MIX = D_ATTN + D_CONV
D_FF = 4 * D_MODEL
EPS = 1e-6

kernel_name = 'mla_shortconv_hybrid_step'


def _rms(x, g):
    xf = x.astype(jnp.float32)
    r = lax.rsqrt(jnp.mean(xf * xf, axis=-1, keepdims=True) + EPS)
    return (xf * r).astype(x.dtype) * g


def _rope(x, pos):
    half = ROPE_DIM // 2
    inv = ROPE_BASE ** (-jnp.arange(0, ROPE_DIM, 2, dtype=jnp.float32) / ROPE_DIM)
    ang = pos.astype(jnp.float32)[:, None] * inv[None, :]
    shape = (1, ang.shape[0]) + (1,) * (x.ndim - 3) + (half,)
    cos = jnp.cos(ang).reshape(shape).astype(x.dtype)
    sin = jnp.sin(ang).reshape(shape).astype(x.dtype)
    x1, x2 = x[..., :half], x[..., half:]
    return jnp.concatenate([x1 * cos - x2 * sin, x1 * sin + x2 * cos], axis=-1)


def _mix_inputs(xn, pos, w_in, q_norm_g, w_uq, kv_norm_g, w_uk):
    b, s, _ = xn.shape
    z = xn @ w_in
    o1 = Q_LORA
    o2 = o1 + KV_LORA
    o3 = o2 + ROPE_DIM
    o4 = o3 + D_CONV
    o5 = o4 + D_CONV
    q_lat, ckv, kpe = z[..., :o1], z[..., o1:o2], z[..., o2:o3]
    b_g, c_g, x_in = z[..., o3:o4], z[..., o4:o5], z[..., o5:]
    q = (_rms(q_lat, q_norm_g) @ w_uq).reshape(b, s, N_HEADS, NOPE_DIM + ROPE_DIM)
    q_nope, q_pe = q[..., :NOPE_DIM], _rope(q[..., NOPE_DIM:], pos)
    q_abs = jnp.einsum('bshn,chn->bshc', q_nope, w_uk)
    ckv = _rms(ckv, kv_norm_g)
    kpe = _rope(kpe, pos)
    return q_abs, q_pe, ckv, kpe, b_g, c_g * x_in


def _prompt_attn(q_abs, q_pe, ckv, kpe):
    b, s, h, c = q_abs.shape
    nb = s // Q_BLOCK
    qa = q_abs.reshape(b, nb, Q_BLOCK, h, c).transpose(1, 0, 2, 3, 4)
    qp = q_pe.reshape(b, nb, Q_BLOCK, h, ROPE_DIM).transpose(1, 0, 2, 3, 4)
    kpos = jnp.arange(s)

    def block(args):
        i, qa_b, qp_b = args
        sc = (jnp.einsum('bqhc,bkc->bhqk', qa_b, ckv)
              + jnp.einsum('bqhr,bkr->bhqk', qp_b, kpe)).astype(jnp.float32) * SCALE
        qpos = i * Q_BLOCK + jnp.arange(Q_BLOCK)
        sc = jnp.where(kpos[None, :] <= qpos[:, None], sc, -jnp.inf)
        p = jax.nn.softmax(sc, axis=-1).astype(ckv.dtype)
        return jnp.einsum('bhqk,bkc->bqhc', p, ckv)

    o = lax.map(block, (jnp.arange(nb), qa, qp))
    return o.transpose(1, 0, 2, 3, 4).reshape(b, s, h, c)


def _online_step(carry, q_abs, q_pe, kc, kp, mask):
    m, l, acc = carry
    sc = (jnp.einsum('bthc,bkc->bhtk', q_abs, kc)
          + jnp.einsum('bthr,bkr->bhtk', q_pe, kp)).astype(jnp.float32) * SCALE
    if mask is not None:
        sc = jnp.where(mask, sc, -jnp.inf)
    m_new = jnp.maximum(m, sc.max(axis=-1))
    corr = jnp.exp(m - m_new)
    p = jnp.exp(sc - m_new[..., None])
    l_new = l * corr + p.sum(axis=-1)
    acc_new = acc * corr[..., None] + jnp.einsum('bhtk,bkc->bhtc', p, kc.astype(jnp.float32))
    return (m_new, l_new, acc_new)


def _sample_attn(q_abs, q_pe, ckv, kpe, cache_ckv, cache_kpe, page_table):
    bd, t, h, c = q_abs.shape
    init = (jnp.full((bd, h, t), -jnp.inf, jnp.float32),
            jnp.zeros((bd, h, t), jnp.float32),
            jnp.zeros((bd, h, t, c), jnp.float32))

    def step(carry, phys):
        return _online_step(carry, q_abs, q_pe, cache_ckv[phys], cache_kpe[phys], None), None

    carry, _ = lax.scan(step, init, page_table.T)
    causal = jnp.tril(jnp.ones((t, t), dtype=bool))[None, None]
    m, l, acc = _online_step(carry, q_abs, q_pe, ckv, kpe, causal)
    o = acc / l[..., None]
    return o.transpose(0, 2, 1, 3).astype(q_abs.dtype)


def _short_conv(u, prev, conv_w):
    s = u.shape[1]
    upad = jnp.concatenate([prev.astype(u.dtype), u], axis=1)
    y = sum(conv_w[k] * upad[:, k:k + s] for k in range(CONV_W))
    return y, upad[:, s:]


def _layer(x, pos, conv_prev, attn_fn, norm1_g, w_in, q_norm_g, w_uq, kv_norm_g,
           w_uk, w_uv, conv_w, w_o, norm2_g, w_up, w_down):
    b, s, _ = x.shape
    xn = _rms(x, norm1_g)
    q_abs, q_pe, ckv, kpe, b_g, u = _mix_inputs(xn, pos, w_in, q_norm_g, w_uq, kv_norm_g, w_uk)
    o_lat = attn_fn(q_abs, q_pe, ckv, kpe)
    attn = jnp.einsum('bshc,chv->bshv', o_lat, w_uv).reshape(b, s, D_ATTN)
    conv_out, conv_state = _short_conv(u, conv_prev, conv_w)
    mix = jnp.concatenate([attn, b_g * conv_out], axis=-1)
    h = x + mix @ w_o
    hn = _rms(h, norm2_g)
    h = h + jnp.square(jax.nn.relu(hn @ w_up)) @ w_down
    return h, ckv, kpe, conv_state


def setup_inputs(seed: int = 0) -> dict:
    key = jax.random.key(seed)
    ks = jax.random.split(key, 20)
    n_pages = PAST_LEN // PAGE_SIZE
    n_used = DEC_BATCH * n_pages
    n_phys = (n_used * 5) // 4
    f32 = jnp.float32

    def nrm(k, shape, fan_in):
        return jax.random.normal(k, shape, f32) * (fan_in ** -0.5)

    def gain(k, shape):
        return 1.0 + 0.05 * jax.random.normal(k, shape, f32)

    page_table = jax.random.permutation(ks[5], n_phys)[:n_used].reshape(DEC_BATCH, n_pages).astype(jnp.int32)
    return {
        'x_prompt': jax.random.normal(ks[0], (BATCH, SEQ, D_MODEL), f32),
        'x_sample': jax.random.normal(ks[1], (DEC_BATCH, DEC_SEQ, D_MODEL), f32),
        'cache_ckv': jax.random.normal(ks[2], (DEPTH, n_phys, PAGE_SIZE, KV_LORA), f32),
        'cache_kpe': jax.random.normal(ks[3], (DEPTH, n_phys, PAGE_SIZE, ROPE_DIM), f32),
        'state_conv': jax.random.normal(ks[4], (DEPTH, DEC_BATCH, CONV_W - 1, D_CONV), f32),
        'page_table': page_table,
        'norm1_g': gain(ks[6], (DEPTH, D_MODEL)),
        'w_in': nrm(ks[7], (DEPTH, D_MODEL, D_IN), D_MODEL),
        'q_norm_g': gain(ks[8], (DEPTH, Q_LORA)),
        'w_uq': nrm(ks[9], (DEPTH, Q_LORA, N_HEADS * (NOPE_DIM + ROPE_DIM)), Q_LORA),
        'kv_norm_g': gain(ks[10], (DEPTH, KV_LORA)),
        'w_uk': nrm(ks[11], (DEPTH, KV_LORA, N_HEADS, NOPE_DIM), KV_LORA),
        'w_uv': nrm(ks[12], (DEPTH, KV_LORA, N_HEADS, V_DIM), KV_LORA),
        'conv_w': nrm(ks[13], (DEPTH, CONV_W, D_CONV), CONV_W),
        'w_o': nrm(ks[14], (DEPTH, D_MIX, D_MODEL), D_MIX),
        'norm2_g': gain(ks[15], (DEPTH, D_MODEL)),
        'w_up': nrm(ks[16], (DEPTH, D_MODEL, D_FF), D_MODEL),
        'w_down': nrm(ks[17], (DEPTH, D_FF, D_MODEL), D_FF),
        'final_g': gain(ks[18], (D_MODEL,)),
    }


def reference(x_prompt, x_sample, cache_ckv, cache_kpe, state_conv, page_table,
              norm1_g, w_in, q_norm_g, w_uq, kv_norm_g, w_uk, w_uv, conv_w, w_o,
              norm2_g, w_up, w_down, final_g):
    seq = x_prompt.shape[1]
    t_new = x_sample.shape[1]
    past = page_table.shape[1] * PAGE_SIZE
    pos_p = jnp.arange(seq, dtype=jnp.int32)
    pos_s = past + jnp.arange(t_new, dtype=jnp.int32)
    zero_conv = jnp.zeros((x_prompt.shape[0], CONV_W - 1, D_CONV), x_prompt.dtype)

    hp, hs = x_prompt, x_sample
    ckv_p, kpe_p, conv_p, ckv_s, kpe_s, conv_s = [], [], [], [], [], []
    for layer in range(DEPTH):
        lw = (norm1_g[layer], w_in[layer], q_norm_g[layer], w_uq[layer], kv_norm_g[layer],
              w_uk[layer], w_uv[layer], conv_w[layer], w_o[layer], norm2_g[layer],
              w_up[layer], w_down[layer])
        hp, a, b, c = _layer(hp, pos_p, zero_conv, _prompt_attn, *lw)
        ckv_p.append(a)
        kpe_p.append(b)
        conv_p.append(c)
        samp_fn = functools.partial(_sample_attn, cache_ckv=cache_ckv[layer],
                                    cache_kpe=cache_kpe[layer], page_table=page_table)
        hs, a, b, c = _layer(hs, pos_s, state_conv[layer], samp_fn, *lw)
        ckv_s.append(a)
        kpe_s.append(b)
        conv_s.append(c)

    y_prompt = _rms(hp, final_g)
    y_sample = _rms(hs, final_g)
    return (y_prompt, y_sample, jnp.stack(ckv_p), jnp.stack(kpe_p), jnp.stack(conv_p),
            jnp.stack(ckv_s), jnp.stack(kpe_s), jnp.stack(conv_s))
```

```python
import functools

import jax
import jax.numpy as jnp
from jax import lax
from jax.experimental import pallas as pl
from jax.experimental.pallas import tpu as pltpu

D_MODEL = 2048
N_HEADS = 8
Q_LORA = 512
KV_LORA = 512
NOPE_DIM = 128
ROPE_DIM = 64
V_DIM = 128
ROPE_BASE = 10000.0
D_ATTN = N_HEADS * V_DIM
D_CONV = D_MODEL - D_ATTN
CONV_W = 3
D_FF = 4 * D_MODEL
EPS = 1e-6
PAGE_SIZE = 128
SCALE = (NOPE_DIM + ROPE_DIM) ** -0.5

LANES = 128
SUBLANES = 8
ROPE_PAD = LANES
QK_DIM = KV_LORA + ROPE_PAD
Q_COLS = NOPE_DIM + 2 * ROPE_PAD
KV_COLS = KV_LORA + 2 * ROPE_PAD
VMEM_LIMIT = 58 * 1024 * 1024

F32 = jnp.float32
BF16 = jnp.bfloat16


def _dot(a, b):
    return jnp.dot(a, b, preferred_element_type=F32)


def _dot_t(a, b):
    return lax.dot_general(a, b, (((1,), (1,)), ((), ())), preferred_element_type=F32)


def _rms(x, g):
    r = lax.rsqrt(jnp.mean(x * x, axis=-1, keepdims=True) + EPS)
    return (x * r) * g


def _resident(shape):
    return pl.BlockSpec(shape, lambda *_: (0,) * len(shape), pipeline_mode=pl.Buffered(1))


def _mix_kernel(seq_len, has_prev, conv_chunk, *refs):
    if has_prev:
        (x_ref, g1_ref, wq_ref, wkv_ref, wc_ref, qg_ref, wuq_ref, kvg_ref, wuk_ref, cw_ref,
         cos_ref, sin_ref, p1_ref, p2_ref,
         qcat_ref, kcat_ref, ckv_ref, kpe_ref, gate_ref, utail_ref, ubuf_ref) = refs
    else:
        (x_ref, g1_ref, wq_ref, wkv_ref, wc_ref, qg_ref, wuq_ref, kvg_ref, wuk_ref, cw_ref,
         cos_ref, sin_ref,
         qcat_ref, kcat_ref, ckv_ref, kpe_ref, gate_ref, utail_ref, ubuf_ref) = refs
    tm = x_ref.shape[0]
    nq, _, tq, _ = qcat_ref.shape
    tail = utail_ref.shape[-2]
    i = pl.program_id(0)

    @pl.when(i == 0)
    def _():
        ubuf_ref[0:SUBLANES, :] = jnp.zeros((SUBLANES, D_CONV), F32)

    xn = _rms(x_ref[...], g1_ref[...]).astype(BF16)
    cos = cos_ref[...]
    sin = sin_ref[...]

    zkv = _dot(xn, wkv_ref[...])
    ckv = _rms(zkv[:, :KV_LORA], kvg_ref[...])
    kpe = zkv[:, KV_LORA:KV_LORA + ROPE_PAD] * cos + zkv[:, KV_LORA + ROPE_PAD:] * sin
    ckv_ref[...] = ckv
    kpe_ref[...] = kpe[:, :ROPE_DIM]
    kcat_ref[:, :KV_LORA] = ckv.astype(BF16)
    kcat_ref[:, KV_LORA:] = kpe.astype(BF16)

    rq = _rms(_dot(xn, wq_ref[...]), qg_ref[...]).astype(BF16)
    for h in range(N_HEADS):
        qh = _dot(rq, wuq_ref[:, h * Q_COLS:(h + 1) * Q_COLS])
        q_abs = _dot(qh[:, :NOPE_DIM].astype(BF16), wuk_ref[h]) * SCALE
        q_pe = (qh[:, NOPE_DIM:NOPE_DIM + ROPE_PAD] * cos + qh[:, NOPE_DIM + ROPE_PAD:] * sin) * SCALE
        qcat_ref[:, h, :, 0:KV_LORA] = q_abs.astype(qcat_ref.dtype).reshape(nq, tq, KV_LORA)
        qcat_ref[:, h, :, KV_LORA:] = q_pe.astype(qcat_ref.dtype).reshape(nq, tq, ROPE_PAD)

    pos = (i * tm + lax.broadcasted_iota(jnp.int32, (tm, 1), 0)) % seq_len
    m1 = pos >= 1
    m2 = pos >= 2
    for j in range(D_CONV // conv_chunk):
        sl = slice(j * conv_chunk, (j + 1) * conv_chunk)
        b_g = _dot(xn, wc_ref[:, sl])
        c_g = _dot(xn, wc_ref[:, D_CONV + j * conv_chunk:D_CONV + (j + 1) * conv_chunk])
        x_in = _dot(xn, wc_ref[:, 2 * D_CONV + j * conv_chunk:2 * D_CONV + (j + 1) * conv_chunk])
        u = c_g * x_in
        ubuf_ref[SUBLANES:SUBLANES + tm, sl] = u
        u1 = ubuf_ref[SUBLANES - 1:SUBLANES - 1 + tm, sl]
        u2 = ubuf_ref[SUBLANES - 2:SUBLANES - 2 + tm, sl]
        if has_prev:
            u1 = jnp.where(m1, u1, p1_ref[:, sl])
            u2 = jnp.where(m2, u2, p2_ref[:, sl])
        else:
            u1 = jnp.where(m1, u1, 0.0)
            u2 = jnp.where(m2, u2, 0.0)
        y = cw_ref[0:1, sl] * u2 + cw_ref[1:2, sl] * u1 + cw_ref[2:3, sl] * u
        gate_ref[:, sl] = (b_g * y).astype(BF16)
    utail_ref[0] = ubuf_ref[SUBLANES + tm - tail:SUBLANES + tm, :]
    ubuf_ref[0:SUBLANES, :] = ubuf_ref[tm:tm + SUBLANES, :]


def _mix_in(x, w, cos, sin, prev, *, seq_len, tm, tq, q_dtype, tail):
    t = x.shape[0]
    nt = t // tm
    nq = tm // tq
    has_prev = prev is not None
    tab_blocks = cos.shape[0] // tm
    row = lambda i: (i, 0)
    in_specs = [
        pl.BlockSpec((tm, D_MODEL), row),
        _resident((1, D_MODEL)),
        _resident((D_MODEL, Q_LORA)),
        _resident((D_MODEL, KV_COLS)),
        _resident((D_MODEL, 3 * D_CONV)),
        _resident((1, Q_LORA)),
        _resident((Q_LORA, N_HEADS * Q_COLS)),
        _resident((1, KV_LORA)),
        _resident((N_HEADS, NOPE_DIM, KV_LORA)),
        _resident((CONV_W, D_CONV)),
        pl.BlockSpec((tm, ROPE_PAD), lambda i: (i % tab_blocks, 0)),
        pl.BlockSpec((tm, ROPE_PAD), lambda i: (i % tab_blocks, 0)),
    ]
    args = [x, w['g1'], w['wq'], w['wkv'], w['wc'], w['qg'], w['wuq'], w['kvg'], w['wuk'], w['cw'],
            cos, sin]
    if has_prev:
        in_specs += [pl.BlockSpec((tm, D_CONV), row), pl.BlockSpec((tm, D_CONV), row)]
        args += list(prev)
    out_shape = (
        jax.ShapeDtypeStruct((t // tq, N_HEADS, tq, QK_DIM), q_dtype),
        jax.ShapeDtypeStruct((t, QK_DIM), BF16),
        jax.ShapeDtypeStruct((t, KV_LORA), F32),
        jax.ShapeDtypeStruct((t, ROPE_DIM), F32),
        jax.ShapeDtypeStruct((t, D_CONV), BF16),
        jax.ShapeDtypeStruct((nt, tail, D_CONV), F32),
    )
    out_specs = (
        pl.BlockSpec((nq, N_HEADS, tq, QK_DIM), lambda i: (i, 0, 0, 0)),
        pl.BlockSpec((tm, QK_DIM), row),
        pl.BlockSpec((tm, KV_LORA), row),
        pl.BlockSpec((tm, ROPE_DIM), row),
        pl.BlockSpec((tm, D_CONV), row),
        pl.BlockSpec((1, tail, D_CONV), lambda i: (i, 0, 0)),
    )
    return pl.pallas_call(
        functools.partial(_mix_kernel, seq_len, has_prev, 256),
        grid=(nt,),
        in_specs=in_specs,
        out_specs=out_specs,
        out_shape=out_shape,
        scratch_shapes=[pltpu.VMEM((tm + SUBLANES, D_CONV), F32)],
        compiler_params=pltpu.CompilerParams(
            dimension_semantics=("arbitrary",), vmem_limit_bytes=VMEM_LIMIT),
        name="mix_in_prev" if has_prev else "mix_in",
    )(*args)


def _prompt_attn_kernel(tk, q_ref, k_ref, wuv_ref, o_ref, m_ref, l_ref, acc_ref):
    _, _, tq, _ = q_ref.shape
    rows = N_HEADS * tq
    qi = pl.program_id(1)
    q = q_ref[0].reshape(rows, QK_DIM)

    m_ref[...] = jnp.full((rows, 1), -jnp.inf, F32)
    l_ref[...] = jnp.zeros((rows, 1), F32)
    acc_ref[...] = jnp.zeros((rows, KV_LORA), F32)

    def step(kt, masked):
        k = k_ref[0, pl.ds(pl.multiple_of(kt * tk, tk), tk), :]
        s = _dot_t(q, k)
        if masked:
            t_row = lax.broadcasted_iota(jnp.int32, (rows, tk), 0) % tq
            t_col = lax.broadcasted_iota(jnp.int32, (rows, tk), 1)
            s = jnp.where(t_col <= t_row, s, -jnp.inf)
        m_old = m_ref[...]
        m_new = jnp.maximum(m_old, jnp.max(s, axis=-1, keepdims=True))
        corr = jnp.exp(m_old - m_new)
        p = jnp.exp(s - m_new)
        l_ref[...] = l_ref[...] * corr + jnp.sum(p, axis=-1, keepdims=True)
        acc_ref[...] = acc_ref[...] * corr + _dot(p.astype(BF16), k[:, :KV_LORA])
        m_ref[...] = m_new

    def full_step(kt, carry):
        step(kt, False)
        return carry

    lax.fori_loop(0, qi, full_step, 0)
    step(qi, True)

    o = acc_ref[...] / l_ref[...]
    for h in range(N_HEADS):
        oh = o[h * tq:(h + 1) * tq].astype(BF16)
        o_ref[:, h * V_DIM:(h + 1) * V_DIM] = _dot(oh, wuv_ref[h]).astype(o_ref.dtype)


def _prompt_attn(qcat, kcat, wuv, *, batch, seq, tq):
    nq = seq // tq
    rows = N_HEADS * tq
    return pl.pallas_call(
        functools.partial(_prompt_attn_kernel, tq),
        grid=(batch, nq),
        in_specs=[
            pl.BlockSpec((1, N_HEADS, tq, QK_DIM), lambda b, i: (b * nq + i, 0, 0, 0)),
            pl.BlockSpec((1, seq, QK_DIM), lambda b, i: (b, 0, 0)),
            _resident((N_HEADS, KV_LORA, V_DIM)),
        ],
        out_specs=pl.BlockSpec((tq, D_ATTN), lambda b, i: (b * nq + i, 0)),
        out_shape=jax.ShapeDtypeStruct((batch * seq, D_ATTN), BF16),
        scratch_shapes=[pltpu.VMEM((rows, 1), F32), pltpu.VMEM((rows, 1), F32),
                        pltpu.VMEM((rows, KV_LORA), F32)],
        compiler_params=pltpu.CompilerParams(
            dimension_semantics=("arbitrary", "arbitrary"), vmem_limit_bytes=VMEM_LIMIT),
        name="prompt_attn",
    )(qcat, kcat.reshape(batch, seq, QK_DIM), wuv)


def _decode_attn_kernel(pages_per_chunk, pt_ref, q_ref, kn_ref, wuv_ref, ckv_hbm, kpe_hbm,
                        o_ref, kbuf, pbuf, sem):
    n_seq, n_pages = pt_ref.shape
    n_chunks = n_pages // pages_per_chunk
    chunk_keys = pages_per_chunk * PAGE_SIZE
    t_new = kn_ref.shape[1]
    rows = N_HEADS * t_new
    b = pl.program_id(0)

    def copies(seq, chunk, slot):
        out = []
        for p in range(pages_per_chunk):
            page = pt_ref[seq, chunk * pages_per_chunk + p]
            out.append(pltpu.make_async_copy(ckv_hbm.at[page], kbuf.at[slot, p], sem.at[slot]))
            out.append(pltpu.make_async_copy(kpe_hbm.at[page], pbuf.at[slot, p], sem.at[slot]))
        return out

    @pl.when(b == 0)
    def _():
        for c in copies(0, 0, 0):
            c.start()

    q = q_ref[0].reshape(rows, QK_DIM).astype(BF16)
    qa = q[:, :KV_LORA]
    qp = q[:, KV_LORA:KV_LORA + ROPE_DIM]

    def online(carry, s, v):
        m, l, acc = carry
        m_new = jnp.maximum(m, jnp.max(s, axis=-1, keepdims=True))
        corr = jnp.exp(m - m_new)
        p = jnp.exp(s - m_new)
        l = l * corr + jnp.sum(p, axis=-1, keepdims=True)
        acc = acc * corr + _dot(p.astype(BF16), v)
        return m_new, l, acc

    def chunk_step(c, carry):
        slot = c % 2
        last = c + 1 == n_chunks
        nxt_seq = jnp.where(last, b + 1, b)
        nxt_chunk = jnp.where(last, 0, c + 1)

        @pl.when(jnp.logical_or(jnp.logical_not(last), b + 1 < n_seq))
        def _():
            for cp in copies(nxt_seq, nxt_chunk, 1 - slot):
                cp.start()

        for cp in copies(b, c, slot):
            cp.wait()
        k = kbuf[slot].reshape(chunk_keys, KV_LORA).astype(BF16)
        kp = pbuf[slot].reshape(chunk_keys, ROPE_DIM).astype(BF16)
        s = _dot_t(qa, k) + _dot_t(qp, kp)
        return online(carry, s, k)

    init = (jnp.full((rows, 1), -jnp.inf, F32), jnp.zeros((rows, 1), F32),
            jnp.zeros((rows, KV_LORA), F32))
    carry = lax.fori_loop(0, n_chunks, chunk_step, init)

    kn = kn_ref[0]
    s = _dot_t(q, kn)
    t_row = lax.broadcasted_iota(jnp.int32, (rows, t_new), 0) % t_new
    t_col = lax.broadcasted_iota(jnp.int32, (rows, t_new), 1)
    s = jnp.where(t_col <= t_row, s, -jnp.inf)
    _, l, acc = online(carry, s, kn[:, :KV_LORA])
    o = acc / l
    for h in range(N_HEADS):
        oh = o[h * t_new:(h + 1) * t_new].astype(BF16)
        o_ref[0, :, h * V_DIM:(h + 1) * V_DIM] = _dot(oh, wuv_ref[h])


def _decode_attn(page_table, qcat, kcat, wuv, cache_ckv, cache_kpe, *, pages_per_chunk):
    n_seq, n_pages = page_table.shape
    t_new = qcat.shape[2]
    assert n_pages % (2 * pages_per_chunk) == 0
    grid_spec = pltpu.PrefetchScalarGridSpec(
        num_scalar_prefetch=1,
        grid=(n_seq,),
        in_specs=[
            pl.BlockSpec((1, N_HEADS, t_new, QK_DIM), lambda b, pt: (b, 0, 0, 0)),
            pl.BlockSpec((1, t_new, QK_DIM), lambda b, pt: (b, 0, 0)),
            pl.BlockSpec((N_HEADS, KV_LORA, V_DIM), lambda b, pt: (0, 0, 0),
                         pipeline_mode=pl.Buffered(1)),
            pl.BlockSpec(memory_space=pl.ANY),
            pl.BlockSpec(memory_space=pl.ANY),
        ],
        out_specs=pl.BlockSpec((1, t_new, D_ATTN), lambda b, pt: (b, 0, 0)),
        scratch_shapes=[
            pltpu.VMEM((2, pages_per_chunk, PAGE_SIZE, KV_LORA), F32),
            pltpu.VMEM((2, pages_per_chunk, PAGE_SIZE, ROPE_DIM), F32),
            pltpu.SemaphoreType.DMA((2,)),
        ],
    )
    return pl.pallas_call(
        functools.partial(_decode_attn_kernel, pages_per_chunk),
        grid_spec=grid_spec,
        out_shape=jax.ShapeDtypeStruct((n_seq, t_new, D_ATTN), F32),
        compiler_params=pltpu.CompilerParams(
            dimension_semantics=("arbitrary",), vmem_limit_bytes=VMEM_LIMIT),
        name="decode_attn",
    )(page_table, qcat, kcat.reshape(n_seq, t_new, QK_DIM), wuv, cache_ckv, cache_kpe)


def _out_mlp_kernel(x_ref, attn_ref, gate_ref, wo_ref, g2_ref, wup_ref, wdn_ref, fg_ref,
                    o_ref, hn_ref):
    f = pl.program_id(1)

    @pl.when(f == 0)
    def _():
        mix_o = (_dot(attn_ref[...].astype(BF16), wo_ref[0:D_ATTN, :])
                 + _dot(gate_ref[...], wo_ref[D_ATTN:, :]))
        h = x_ref[...] + mix_o
        o_ref[...] = h
        hn_ref[...] = _rms(h, g2_ref[...]).astype(BF16)

    a = _dot(hn_ref[...], wup_ref[...])
    a = jnp.square(jnp.maximum(a, 0.0)).astype(BF16)
    o_ref[...] += _dot(a, wdn_ref[...])

    @pl.when(f == pl.num_programs(1) - 1)
    def _():
        o_ref[...] = _rms(o_ref[...], fg_ref[...])


def _out_mlp(x, attn, gate, w, *, tm, tf):
    t = x.shape[0]
    row = lambda i, f: (i, 0)
    return pl.pallas_call(
        _out_mlp_kernel,
        grid=(t // tm, D_FF // tf),
        in_specs=[
            pl.BlockSpec((tm, D_MODEL), row),
            pl.BlockSpec((tm, D_ATTN), row),
            pl.BlockSpec((tm, D_CONV), row),
            _resident((D_MODEL, D_MODEL)),
            _resident((1, D_MODEL)),
            pl.BlockSpec((D_MODEL, tf), lambda i, f: (0, f)),
            pl.BlockSpec((tf, D_MODEL), lambda i, f: (f, 0)),
            _resident((1, D_MODEL)),
        ],
        out_specs=pl.BlockSpec((tm, D_MODEL), row),
        out_shape=jax.ShapeDtypeStruct((t, D_MODEL), F32),
        scratch_shapes=[pltpu.VMEM((tm, D_MODEL), BF16)],
        compiler_params=pltpu.CompilerParams(
            dimension_semantics=("arbitrary", "arbitrary"), vmem_limit_bytes=VMEM_LIMIT),
        name="out_mlp",
    )(x, attn, gate, w['wo'], w['g2'], w['wup'], w['wdn'], w['fg'])


def _pad_lanes(a):
    return jnp.concatenate([a, jnp.zeros(a.shape[:-1] + (ROPE_PAD - a.shape[-1],), a.dtype)], -1)


def _rot_cols(a):
    half = ROPE_DIM // 2
    return jnp.concatenate([-a[..., half:], a[..., :half]], -1)


def _layer_weights(norm1_g, w_in, q_norm_g, w_uq, kv_norm_g, w_uk, w_uv, conv_w, w_o, norm2_g,
                   w_up, w_down, final_g):
    o1 = Q_LORA
    o2 = o1 + KV_LORA
    o3 = o2 + ROPE_DIM
    w_kpe = w_in[:, o2:o3]
    wkv = jnp.concatenate([w_in[:, o1:o2], _pad_lanes(w_kpe), _pad_lanes(_rot_cols(w_kpe))], -1)
    wuq = w_uq.reshape(Q_LORA, N_HEADS, NOPE_DIM + ROPE_DIM)
    wuq_pe = wuq[..., NOPE_DIM:]
    wuq = jnp.concatenate([wuq[..., :NOPE_DIM], _pad_lanes(wuq_pe), _pad_lanes(_rot_cols(wuq_pe))], -1)
    return {
        'g1': norm1_g.reshape(1, D_MODEL),
        'wq': w_in[:, :o1].astype(BF16),
        'wkv': wkv.astype(BF16),
        'wc': w_in[:, o3:].astype(BF16),
        'qg': q_norm_g.reshape(1, Q_LORA),
        'wuq': wuq.reshape(Q_LORA, N_HEADS * Q_COLS).astype(BF16),
        'kvg': kv_norm_g.reshape(1, KV_LORA),
        'wuk': w_uk.transpose(1, 2, 0).astype(BF16),
        'wuv': w_uv.transpose(1, 0, 2).astype(BF16),
        'cw': conv_w,
        'wo': w_o.astype(BF16),
        'g2': norm2_g.reshape(1, D_MODEL),
        'wup': w_up.astype(BF16),
        'wdn': w_down.astype(BF16),
        'fg': final_g.reshape(1, D_MODEL),
    }


def _rope_tables(pos):
    inv = ROPE_BASE ** (-jnp.arange(0, ROPE_DIM, 2, dtype=F32) / ROPE_DIM)
    ang = pos.astype(F32)[:, None] * inv[None, :]
    cos, sin = jnp.cos(ang), jnp.sin(ang)
    return _pad_lanes(jnp.concatenate([cos, cos], -1)), _pad_lanes(jnp.concatenate([sin, sin], -1))


def kernel(x_prompt, x_sample, cache_ckv, cache_kpe, state_conv, page_table, norm1_g, w_in,
           q_norm_g, w_uq, kv_norm_g, w_uk, w_uv, conv_w, w_o, norm2_g, w_up, w_down, final_g):
    batch, seq, _ = x_prompt.shape
    dec_batch, t_new, _ = x_sample.shape
    depth = norm1_g.shape[0]
    assert depth == 1
    past = page_table.shape[1] * PAGE_SIZE
    tm_mix, tq, tm_mlp, tf = 256, 256, 512, 1024

    w = _layer_weights(norm1_g[0], w_in[0], q_norm_g[0], w_uq[0], kv_norm_g[0], w_uk[0], w_uv[0],
                       conv_w[0], w_o[0], norm2_g[0], w_up[0], w_down[0], final_g)

    xp = x_prompt.reshape(batch * seq, D_MODEL)
    cos_p, sin_p = _rope_tables(jnp.arange(seq, dtype=jnp.int32))
    q_p, k_p, ckv_p, kpe_p, gate_p, utail_p = _mix_in(
        xp, w, cos_p, sin_p, None, seq_len=seq, tm=tm_mix, tq=tq, q_dtype=BF16, tail=SUBLANES)
    attn_p = _prompt_attn(q_p, k_p, w['wuv'], batch=batch, seq=seq, tq=tq)
    y_p = _out_mlp(xp, attn_p, gate_p, w, tm=tm_mlp, tf=tf)
    conv_p = utail_p.reshape(batch, seq // tm_mix, SUBLANES, D_CONV)[:, -1, SUBLANES - (CONV_W - 1):]

    xs = x_sample.reshape(dec_batch * t_new, D_MODEL)
    cos_s, sin_s = _rope_tables(past + jnp.arange(t_new, dtype=jnp.int32))
    reps = tm_mix // t_new
    cos_s, sin_s = jnp.tile(cos_s, (reps, 1)), jnp.tile(sin_s, (reps, 1))
    st = state_conv[0]
    zeros = jnp.zeros((dec_batch, t_new, D_CONV), F32)
    prev1 = zeros.at[:, 0].set(st[:, 1]).reshape(dec_batch * t_new, D_CONV)
    prev2 = zeros.at[:, 0].set(st[:, 0]).at[:, 1].set(st[:, 1]).reshape(dec_batch * t_new, D_CONV)
    q_s, k_s, ckv_s, kpe_s, gate_s, utail_s = _mix_in(
        xs, w, cos_s, sin_s, (prev1, prev2), seq_len=t_new, tm=tm_mix, tq=t_new, q_dtype=F32,
        tail=tm_mix)
    attn_s = _decode_attn(page_table, q_s, k_s, w['wuv'],
                          cache_ckv.reshape(cache_ckv.shape[1:]), cache_kpe.reshape(cache_kpe.shape[1:]),
                          pages_per_chunk=8)
    y_s = _out_mlp(xs, attn_s.reshape(dec_batch * t_new, D_ATTN), gate_s, w, tm=tm_mlp, tf=tf)
    conv_s = utail_s.reshape(dec_batch, t_new, D_CONV)[:, t_new - (CONV_W - 1):]

    return (y_p.reshape(batch, seq, D_MODEL),
            y_s.reshape(dec_batch, t_new, D_MODEL),
            ckv_p.reshape(depth, batch, seq, KV_LORA),
            kpe_p.reshape(depth, batch, seq, ROPE_DIM),
            conv_p.reshape(depth, batch, CONV_W - 1, D_CONV),
            ckv_s.reshape(depth, dec_batch, t_new, KV_LORA),
            kpe_s.reshape(depth, dec_batch, t_new, ROPE_DIM),
            conv_s.reshape(depth, dec_batch, CONV_W - 1, D_CONV))
```

```python
import functools

import jax
import jax.numpy as jnp
from jax import lax
from jax.experimental import pallas as pl
from jax.experimental.pallas import tpu as pltpu

D_MODEL = 2048
N_HEADS = 8
Q_LORA = 512
KV_LORA = 512
NOPE_DIM = 128
ROPE_DIM = 64
V_DIM = 128
ROPE_BASE = 10000.0
D_ATTN = N_HEADS * V_DIM
D_CONV = D_MODEL - D_ATTN
CONV_W = 3
D_FF = 4 * D_MODEL
EPS = 1e-6
PAGE_SIZE = 128
SCALE = (NOPE_DIM + ROPE_DIM) ** -0.5

LANES = 128
SUBLANES = 8
ROPE_PAD = LANES
QK_DIM = KV_LORA + ROPE_PAD
Q_COLS = NOPE_DIM + 2 * ROPE_PAD
KV_COLS = KV_LORA + 2 * ROPE_PAD
VMEM_LIMIT = 58 * 1024 * 1024

F32 = jnp.float32
BF16 = jnp.bfloat16


def _dot(a, b):
    return jnp.dot(a, b, preferred_element_type=F32)


def _dot_t(a, b):
    return lax.dot_general(a, b, (((1,), (1,)), ((), ())), preferred_element_type=F32)


def _rms(x, g):
    r = lax.rsqrt(jnp.mean(x * x, axis=-1, keepdims=True) + EPS)
    return (x * r) * g


def _resident(shape):
    return pl.BlockSpec(shape, lambda *_: (0,) * len(shape), pipeline_mode=pl.Buffered(1))


def _store_transposed(dst_ref, latent, rope):
    n_tiles, _, tq = dst_ref.shape
    for t in range(n_tiles):
        rows = slice(t * tq, (t + 1) * tq)
        for j in range(KV_LORA // LANES):
            cols = slice(j * LANES, (j + 1) * LANES)
            dst_ref[t, cols, :] = latent[rows, cols].T.astype(BF16)
        dst_ref[t, KV_LORA:, :] = rope[rows].T.astype(BF16)


def _mix_kernel(seq_len, has_prev, conv_chunk, *refs):
    if has_prev:
        (x_ref, g1_ref, wq_ref, wkv_ref, wc_ref, qg_ref, wuq_ref, kvg_ref, wuk_ref, cw_ref,
         cos_ref, sin_ref, p1_ref, p2_ref,
         qcat_ref, kcat_ref, ckv_ref, kpe_ref, gate_ref, utail_ref, ubuf_ref) = refs
    else:
        (x_ref, g1_ref, wq_ref, wkv_ref, wc_ref, qg_ref, wuq_ref, kvg_ref, wuk_ref, cw_ref,
         cos_ref, sin_ref,
         qcat_ref, kcat_ref, ckv_ref, kpe_ref, gate_ref, utail_ref, kt_ref, ubuf_ref) = refs
    tm = x_ref.shape[0]
    tail = utail_ref.shape[-2]
    i = pl.program_id(0)

    @pl.when(i == 0)
    def _():
        ubuf_ref[0:SUBLANES, :] = jnp.zeros((SUBLANES, D_CONV), F32)

    xn = _rms(x_ref[...], g1_ref[...]).astype(BF16)
    cos = cos_ref[...]
    sin = sin_ref[...]

    zkv = _dot(xn, wkv_ref[...])
    ckv = _rms(zkv[:, :KV_LORA], kvg_ref[...])
    kpe = zkv[:, KV_LORA:KV_LORA + ROPE_PAD] * cos + zkv[:, KV_LORA + ROPE_PAD:] * sin
    ckv_ref[...] = ckv
    kpe_ref[...] = kpe[:, :ROPE_DIM]
    kcat_ref[:, :KV_LORA] = ckv.astype(BF16)
    kcat_ref[:, KV_LORA:] = kpe.astype(BF16)
    if not has_prev:
        _store_transposed(kt_ref, ckv, kpe)

    rq = _rms(_dot(xn, wq_ref[...]), qg_ref[...]).astype(BF16)
    for h in range(N_HEADS):
        qh = _dot(rq, wuq_ref[:, h * Q_COLS:(h + 1) * Q_COLS])
        q_abs = _dot(qh[:, :NOPE_DIM].astype(BF16), wuk_ref[h]) * SCALE
        q_pe = (qh[:, NOPE_DIM:NOPE_DIM + ROPE_PAD] * cos + qh[:, NOPE_DIM + ROPE_PAD:] * sin) * SCALE
        if has_prev:
            nq, _, tq, _ = qcat_ref.shape
            qcat_ref[:, h, :, 0:KV_LORA] = q_abs.reshape(nq, tq, KV_LORA)
            qcat_ref[:, h, :, KV_LORA:] = q_pe.reshape(nq, tq, ROPE_PAD)
        else:
            _store_transposed(qcat_ref.at[:, h], q_abs, q_pe)

    pos = (i * tm + lax.broadcasted_iota(jnp.int32, (tm, 1), 0)) % seq_len
    m1 = pos >= 1
    m2 = pos >= 2
    for j in range(D_CONV // conv_chunk):
        sl = slice(j * conv_chunk, (j + 1) * conv_chunk)
        b_g = _dot(xn, wc_ref[:, sl])
        c_g = _dot(xn, wc_ref[:, D_CONV + j * conv_chunk:D_CONV + (j + 1) * conv_chunk])
        x_in = _dot(xn, wc_ref[:, 2 * D_CONV + j * conv_chunk:2 * D_CONV + (j + 1) * conv_chunk])
        u = c_g * x_in
        ubuf_ref[SUBLANES:SUBLANES + tm, sl] = u
        u1 = ubuf_ref[SUBLANES - 1:SUBLANES - 1 + tm, sl]
        u2 = ubuf_ref[SUBLANES - 2:SUBLANES - 2 + tm, sl]
        if has_prev:
            u1 = jnp.where(m1, u1, p1_ref[:, sl])
            u2 = jnp.where(m2, u2, p2_ref[:, sl])
        else:
            u1 = jnp.where(m1, u1, 0.0)
            u2 = jnp.where(m2, u2, 0.0)
        y = cw_ref[0:1, sl] * u2 + cw_ref[1:2, sl] * u1 + cw_ref[2:3, sl] * u
        gate_ref[:, sl] = (b_g * y).astype(BF16)
    utail_ref[0] = ubuf_ref[SUBLANES + tm - tail:SUBLANES + tm, :]
    ubuf_ref[0:SUBLANES, :] = ubuf_ref[tm:tm + SUBLANES, :]


def _mix_in(x, w, cos, sin, prev, *, seq_len, tm, tail, tq=None):
    t = x.shape[0]
    nt = t // tm
    has_prev = prev is not None
    tab_blocks = cos.shape[0] // tm
    row = lambda i: (i, 0)
    in_specs = [
        pl.BlockSpec((tm, D_MODEL), row),
        _resident((1, D_MODEL)),
        _resident((D_MODEL, Q_LORA)),
        _resident((D_MODEL, KV_COLS)),
        _resident((D_MODEL, 3 * D_CONV)),
        _resident((1, Q_LORA)),
        _resident((Q_LORA, N_HEADS * Q_COLS)),
        _resident((1, KV_LORA)),
        _resident((N_HEADS, NOPE_DIM, KV_LORA)),
        _resident((CONV_W, D_CONV)),
        pl.BlockSpec((tm, ROPE_PAD), lambda i: (i % tab_blocks, 0)),
        pl.BlockSpec((tm, ROPE_PAD), lambda i: (i % tab_blocks, 0)),
    ]
    args = [x, w['g1'], w['wq'], w['wkv'], w['wc'], w['qg'], w['wuq'], w['kvg'], w['wuk'], w['cw'],
            cos, sin]
    if has_prev:
        in_specs += [pl.BlockSpec((tm, D_CONV), row), pl.BlockSpec((tm, D_CONV), row)]
        args += list(prev)
    if has_prev:
        q_shape = jax.ShapeDtypeStruct((t // seq_len, N_HEADS, seq_len, QK_DIM), F32)
        q_spec = pl.BlockSpec((tm // seq_len, N_HEADS, seq_len, QK_DIM), lambda i: (i, 0, 0, 0))
    else:
        q_shape = jax.ShapeDtypeStruct((t // tq, N_HEADS, QK_DIM, tq), BF16)
        q_spec = pl.BlockSpec((tm // tq, N_HEADS, QK_DIM, tq), lambda i: (i, 0, 0, 0))
    out_shape = [
        q_shape,
        jax.ShapeDtypeStruct((t, QK_DIM), BF16),
        jax.ShapeDtypeStruct((t, KV_LORA), F32),
        jax.ShapeDtypeStruct((t, ROPE_DIM), F32),
        jax.ShapeDtypeStruct((t, D_CONV), BF16),
        jax.ShapeDtypeStruct((nt, tail, D_CONV), F32),
    ]
    out_specs = [
        q_spec,
        pl.BlockSpec((tm, QK_DIM), row),
        pl.BlockSpec((tm, KV_LORA), row),
        pl.BlockSpec((tm, ROPE_DIM), row),
        pl.BlockSpec((tm, D_CONV), row),
        pl.BlockSpec((1, tail, D_CONV), lambda i: (i, 0, 0)),
    ]
    if not has_prev:
        out_shape.append(jax.ShapeDtypeStruct((t // tq, QK_DIM, tq), BF16))
        out_specs.append(pl.BlockSpec((tm // tq, QK_DIM, tq), lambda i: (i, 0, 0)))
    return pl.pallas_call(
        functools.partial(_mix_kernel, seq_len, has_prev, 256),
        grid=(nt,),
        in_specs=in_specs,
        out_specs=out_specs,
        out_shape=out_shape,
        scratch_shapes=[pltpu.VMEM((tm + SUBLANES, D_CONV), F32)],
        compiler_params=pltpu.CompilerParams(
            dimension_semantics=("arbitrary",), vmem_limit_bytes=VMEM_LIMIT),
        name="mix_in_prev" if has_prev else "mix_in",
    )(*args)


def _prompt_attn_kernel(qt_ref, kt_ref, k_ref, wuvt_ref, o_ref, m_ref, l_ref, acc_ref):
    tq = qt_ref.shape[-1]
    tk = k_ref.shape[1]
    qi = pl.program_id(1)

    m_ref[...] = jnp.full(m_ref.shape, -jnp.inf, F32)
    l_ref[...] = jnp.zeros(l_ref.shape, F32)
    acc_ref[...] = jnp.zeros(acc_ref.shape, F32)

    def tile_step(kt, masked):
        k = k_ref[kt]
        v_t = kt_ref[kt, 0:KV_LORA, :]
        if masked:
            visible = (lax.broadcasted_iota(jnp.int32, (tk, tq), 0)
                       <= lax.broadcasted_iota(jnp.int32, (tk, tq), 1))
        scores = [_dot(k, qt_ref[0, h]) for h in range(N_HEADS)]
        for h in range(N_HEADS):
            s_t = scores[h]
            if masked:
                s_t = jnp.where(visible, s_t, -jnp.inf)
            m_old = m_ref[h]
            m_new = jnp.maximum(m_old, jnp.max(s_t, axis=0, keepdims=True))
            corr = jnp.exp(m_old - m_new)
            p_t = jnp.exp(s_t - m_new)
            l_ref[h] = l_ref[h] * corr + jnp.sum(p_t, axis=0, keepdims=True)
            acc_ref[h] = acc_ref[h] * corr + _dot(v_t, p_t.astype(BF16))
            m_ref[h] = m_new

    def full_step(kt, carry):
        tile_step(kt, False)
        return carry

    lax.fori_loop(0, qi, full_step, 0)
    tile_step(qi, True)

    for h in range(N_HEADS):
        o_t = (acc_ref[h] * (1.0 / l_ref[h])).astype(BF16)
        a_t = _dot(wuvt_ref[h], o_t)
        o_ref[:, h * V_DIM:(h + 1) * V_DIM] = a_t.T.astype(o_ref.dtype)


def _prompt_attn(q_t, kcat_t, kcat, wuv_t, *, batch, seq, tq):
    nq = seq // tq
    return pl.pallas_call(
        _prompt_attn_kernel,
        grid=(batch, nq),
        in_specs=[
            pl.BlockSpec((1, N_HEADS, QK_DIM, tq), lambda b, i: (b * nq + i, 0, 0, 0)),
            pl.BlockSpec((nq, QK_DIM, tq), lambda b, i: (b, 0, 0)),
            pl.BlockSpec((nq, tq, QK_DIM), lambda b, i: (b, 0, 0)),
            _resident((N_HEADS, V_DIM, KV_LORA)),
        ],
        out_specs=pl.BlockSpec((tq, D_ATTN), lambda b, i: (b * nq + i, 0)),
        out_shape=jax.ShapeDtypeStruct((batch * seq, D_ATTN), BF16),
        scratch_shapes=[pltpu.VMEM((N_HEADS, 1, tq), F32), pltpu.VMEM((N_HEADS, 1, tq), F32),
                        pltpu.VMEM((N_HEADS, KV_LORA, tq), F32)],
        compiler_params=pltpu.CompilerParams(
            dimension_semantics=("arbitrary", "arbitrary"), vmem_limit_bytes=VMEM_LIMIT),
        name="prompt_attn",
    )(q_t, kcat_t, kcat.reshape(batch * nq, tq, QK_DIM), wuv_t)


def _decode_attn_kernel(pages_per_dot, pt_ref, q_ref, kn_ref, ckv_hbm, kpet_hbm,
                        o_ref, kbuf, pbuf, sem):
    n_seq, n_pages = pt_ref.shape
    n_slots, pages_per_chunk = kbuf.shape[0], kbuf.shape[1]
    n_chunks = n_pages // pages_per_chunk
    ahead = n_slots - 1
    dot_keys = pages_per_dot * PAGE_SIZE
    t_new = kn_ref.shape[1]
    rows = N_HEADS * t_new
    b = pl.program_id(0)

    def copies(g):
        seq, chunk, slot = g // n_chunks, g % n_chunks, g % n_slots
        out = []
        for p in range(pages_per_chunk):
            page = pt_ref[seq, chunk * pages_per_chunk + p]
            out.append(pltpu.make_async_copy(ckv_hbm.at[page], kbuf.at[slot, p], sem.at[slot]))
            out.append(pltpu.make_async_copy(kpet_hbm.at[page], pbuf.at[slot, p], sem.at[slot]))
        return out

    @pl.when(b == 0)
    def _():
        for g in range(ahead):
            for c in copies(g):
                c.start()

    q = q_ref[0].reshape(rows, QK_DIM).astype(BF16)
    qa = q[:, :KV_LORA]
    qp = q[:, KV_LORA:KV_LORA + ROPE_DIM]

    kn = kn_ref[0]
    s = _dot_t(q, kn)
    t_row = lax.broadcasted_iota(jnp.int32, (rows, t_new), 0) % t_new
    t_col = lax.broadcasted_iota(jnp.int32, (rows, t_new), 1)
    s = jnp.where(t_col <= t_row, s, -jnp.inf)
    m0 = jnp.max(s, axis=-1, keepdims=True)
    p0 = jnp.exp(s - m0)
    init = (m0, jnp.sum(p0, axis=-1, keepdims=True), _dot(p0.astype(BF16), kn[:, :KV_LORA]))

    def chunk_step(c, carry):
        m, l, acc = carry
        g = b * n_chunks + c
        slot = g % n_slots

        @pl.when(g + ahead < n_seq * n_chunks)
        def _():
            for cp in copies(g + ahead):
                cp.start()

        for cp in copies(g):
            cp.wait()
        keys, scores = [], []
        for j in range(pages_per_chunk // pages_per_dot):
            pages = range(j * pages_per_dot, (j + 1) * pages_per_dot)
            k = kbuf[slot, j * pages_per_dot:(j + 1) * pages_per_dot].reshape(dot_keys, KV_LORA)
            k = k.astype(BF16)
            kp_t = jnp.concatenate([pbuf[slot, p] for p in pages], axis=1).astype(BF16)
            keys.append(k)
            scores.append(_dot_t(qa, k) + _dot(qp, kp_t))
        for k, s in zip(keys, scores):
            m_new = jnp.maximum(m, jnp.max(s, axis=-1, keepdims=True))
            corr = jnp.exp(m - m_new)
            p = jnp.exp(s - m_new)
            l = l * corr + jnp.sum(p, axis=-1, keepdims=True)
            acc = acc * corr + _dot(p.astype(BF16), k)
            m = m_new
        return m, l, acc

    _, l, acc = lax.fori_loop(0, n_chunks, chunk_step, init)
    o_ref[0] = acc * (1.0 / l)


def _decode_attn(page_table, qcat, kcat, cache_ckv, cache_kpe_t, *, pages_per_chunk, pages_per_dot,
                 n_slots):
    n_seq, n_pages = page_table.shape
    t_new = qcat.shape[2]
    rows = N_HEADS * t_new
    assert n_pages % pages_per_chunk == 0 and pages_per_chunk % pages_per_dot == 0
    assert n_slots - 1 <= n_seq * (n_pages // pages_per_chunk)
    grid_spec = pltpu.PrefetchScalarGridSpec(
        num_scalar_prefetch=1,
        grid=(n_seq,),
        in_specs=[
            pl.BlockSpec((1, N_HEADS, t_new, QK_DIM), lambda b, pt: (b, 0, 0, 0)),
            pl.BlockSpec((1, t_new, QK_DIM), lambda b, pt: (b, 0, 0)),
            pl.BlockSpec(memory_space=pl.ANY),
            pl.BlockSpec(memory_space=pl.ANY),
        ],
        out_specs=pl.BlockSpec((1, rows, KV_LORA), lambda b, pt: (b, 0, 0)),
        scratch_shapes=[
            pltpu.VMEM((n_slots, pages_per_chunk, PAGE_SIZE, KV_LORA), F32),
            pltpu.VMEM((n_slots, pages_per_chunk, ROPE_DIM, PAGE_SIZE), F32),
            pltpu.SemaphoreType.DMA((n_slots,)),
        ],
    )
    return pl.pallas_call(
        functools.partial(_decode_attn_kernel, pages_per_dot),
        grid_spec=grid_spec,
        out_shape=jax.ShapeDtypeStruct((n_seq, rows, KV_LORA), F32),
        compiler_params=pltpu.CompilerParams(
            dimension_semantics=("arbitrary",), vmem_limit_bytes=VMEM_LIMIT),
        name="decode_attn",
    )(page_table, qcat, kcat.reshape(n_seq, t_new, QK_DIM), cache_ckv, cache_kpe_t)


def _uv_proj_kernel(o_ref, wuv_ref, a_ref):
    n_seq, _, t_new, _ = o_ref.shape
    o = o_ref[:, 0].reshape(n_seq * t_new, KV_LORA).astype(BF16)
    a_ref[...] = _dot(o, wuv_ref[0])


def _uv_proj(o_lat, wuv, *, t_new):
    n_seq = o_lat.shape[0]
    return pl.pallas_call(
        _uv_proj_kernel,
        grid=(N_HEADS,),
        in_specs=[
            pl.BlockSpec((n_seq, 1, t_new, KV_LORA), lambda h: (0, h, 0, 0)),
            pl.BlockSpec((1, KV_LORA, V_DIM), lambda h: (h, 0, 0)),
        ],
        out_specs=pl.BlockSpec((n_seq * t_new, V_DIM), lambda h: (0, h)),
        out_shape=jax.ShapeDtypeStruct((n_seq * t_new, D_ATTN), F32),
        compiler_params=pltpu.CompilerParams(dimension_semantics=("arbitrary",)),
        name="uv_proj",
    )(o_lat.reshape(n_seq, N_HEADS, t_new, KV_LORA), wuv)


def _out_mlp_kernel(x_ref, attn_ref, gate_ref, wo_ref, g2_ref, wup_ref, wdn_ref, fg_ref,
                    o_ref, hn_ref):
    f = pl.program_id(1)

    @pl.when(f == 0)
    def _():
        mix_o = (_dot(attn_ref[...].astype(BF16), wo_ref[0:D_ATTN, :])
                 + _dot(gate_ref[...], wo_ref[D_ATTN:, :]))
        h = x_ref[...] + mix_o
        o_ref[...] = h
        hn_ref[...] = _rms(h, g2_ref[...]).astype(BF16)

    a = _dot(hn_ref[...], wup_ref[...])
    a = jnp.square(jnp.maximum(a, 0.0)).astype(BF16)
    o_ref[...] += _dot(a, wdn_ref[...])

    @pl.when(f == pl.num_programs(1) - 1)
    def _():
        o_ref[...] = _rms(o_ref[...], fg_ref[...])


def _out_mlp(x, attn, gate, w, *, tm, tf):
    t = x.shape[0]
    row = lambda i, f: (i, 0)
    return pl.pallas_call(
        _out_mlp_kernel,
        grid=(t // tm, D_FF // tf),
        in_specs=[
            pl.BlockSpec((tm, D_MODEL), row),
            pl.BlockSpec((tm, D_ATTN), row),
            pl.BlockSpec((tm, D_CONV), row),
            _resident((D_MODEL, D_MODEL)),
            _resident((1, D_MODEL)),
            pl.BlockSpec((D_MODEL, tf), lambda i, f: (0, f)),
            pl.BlockSpec((tf, D_MODEL), lambda i, f: (f, 0)),
            _resident((1, D_MODEL)),
        ],
        out_specs=pl.BlockSpec((tm, D_MODEL), row),
        out_shape=jax.ShapeDtypeStruct((t, D_MODEL), F32),
        scratch_shapes=[pltpu.VMEM((tm, D_MODEL), BF16)],
        compiler_params=pltpu.CompilerParams(
            dimension_semantics=("arbitrary", "arbitrary"), vmem_limit_bytes=VMEM_LIMIT),
        name="out_mlp",
    )(x, attn, gate, w['wo'], w['g2'], w['wup'], w['wdn'], w['fg'])


def _pad_lanes(a):
    return jnp.concatenate([a, jnp.zeros(a.shape[:-1] + (ROPE_PAD - a.shape[-1],), a.dtype)], -1)


def _rot_cols(a):
    half = ROPE_DIM // 2
    return jnp.concatenate([-a[..., half:], a[..., :half]], -1)


def _layer_weights(norm1_g, w_in, q_norm_g, w_uq, kv_norm_g, w_uk, w_uv, conv_w, w_o, norm2_g,
                   w_up, w_down, final_g):
    o1 = Q_LORA
    o2 = o1 + KV_LORA
    o3 = o2 + ROPE_DIM
    w_kpe = w_in[:, o2:o3]
    wkv = jnp.concatenate([w_in[:, o1:o2], _pad_lanes(w_kpe), _pad_lanes(_rot_cols(w_kpe))], -1)
    wuq = w_uq.reshape(Q_LORA, N_HEADS, NOPE_DIM + ROPE_DIM)
    wuq_pe = wuq[..., NOPE_DIM:]
    wuq = jnp.concatenate([wuq[..., :NOPE_DIM], _pad_lanes(wuq_pe), _pad_lanes(_rot_cols(wuq_pe))], -1)
    return {
        'g1': norm1_g.reshape(1, D_MODEL),
        'wq': w_in[:, :o1].astype(BF16),
        'wkv': wkv.astype(BF16),
        'wc': w_in[:, o3:].astype(BF16),
        'qg': q_norm_g.reshape(1, Q_LORA),
        'wuq': wuq.reshape(Q_LORA, N_HEADS * Q_COLS).astype(BF16),
        'kvg': kv_norm_g.reshape(1, KV_LORA),
        'wuk': w_uk.transpose(1, 2, 0).astype(BF16),
        'wuv': w_uv.transpose(1, 0, 2).astype(BF16),
        'wuv_t': w_uv.transpose(1, 2, 0).astype(BF16),
        'cw': conv_w,
        'wo': w_o.astype(BF16),
        'g2': norm2_g.reshape(1, D_MODEL),
        'wup': w_up.astype(BF16),
        'wdn': w_down.astype(BF16),
        'fg': final_g.reshape(1, D_MODEL),
    }


def _rope_tables(pos):
    inv = ROPE_BASE ** (-jnp.arange(0, ROPE_DIM, 2, dtype=F32) / ROPE_DIM)
    ang = pos.astype(F32)[:, None] * inv[None, :]
    cos, sin = jnp.cos(ang), jnp.sin(ang)
    return _pad_lanes(jnp.concatenate([cos, cos], -1)), _pad_lanes(jnp.concatenate([sin, sin], -1))


def kernel(x_prompt, x_sample, cache_ckv, cache_kpe, state_conv, page_table, norm1_g, w_in,
           q_norm_g, w_uq, kv_norm_g, w_uk, w_uv, conv_w, w_o, norm2_g, w_up, w_down, final_g):
    batch, seq, _ = x_prompt.shape
    dec_batch, t_new, _ = x_sample.shape
    depth = norm1_g.shape[0]
    assert depth == 1
    past = page_table.shape[1] * PAGE_SIZE
    tm_mix_p, tm_mix_s, tq, tm_mlp, tf = 512, 256, 256, 512, 1024

    w = _layer_weights(norm1_g[0], w_in[0], q_norm_g[0], w_uq[0], kv_norm_g[0], w_uk[0], w_uv[0],
                       conv_w[0], w_o[0], norm2_g[0], w_up[0], w_down[0], final_g)

    xp = x_prompt.reshape(batch * seq, D_MODEL)
    cos_p, sin_p = _rope_tables(jnp.arange(seq, dtype=jnp.int32))
    qt_p, k_p, ckv_p, kpe_p, gate_p, utail_p, kt_p = _mix_in(
        xp, w, cos_p, sin_p, None, seq_len=seq, tm=tm_mix_p, tail=SUBLANES, tq=tq)
    attn_p = _prompt_attn(qt_p, kt_p, k_p, w['wuv_t'], batch=batch, seq=seq, tq=tq)
    y_p = _out_mlp(xp, attn_p, gate_p, w, tm=tm_mlp, tf=tf)
    conv_p = utail_p.reshape(batch, seq // tm_mix_p, SUBLANES, D_CONV)[:, -1, SUBLANES - (CONV_W - 1):]

    xs = x_sample.reshape(dec_batch * t_new, D_MODEL)
    cos_s, sin_s = _rope_tables(past + jnp.arange(t_new, dtype=jnp.int32))
    reps = tm_mix_s // t_new
    cos_s, sin_s = jnp.tile(cos_s, (reps, 1)), jnp.tile(sin_s, (reps, 1))
    st = state_conv[0]
    zeros = jnp.zeros((dec_batch, t_new, D_CONV), F32)
    prev1 = zeros.at[:, 0].set(st[:, 1]).reshape(dec_batch * t_new, D_CONV)
    prev2 = zeros.at[:, 0].set(st[:, 0]).at[:, 1].set(st[:, 1]).reshape(dec_batch * t_new, D_CONV)
    q_s, k_s, ckv_s, kpe_s, gate_s, utail_s = _mix_in(
        xs, w, cos_s, sin_s, (prev1, prev2), seq_len=t_new, tm=tm_mix_s, tail=tm_mix_s)
    kpe_t = jnp.swapaxes(cache_kpe.reshape(cache_kpe.shape[1:]), 1, 2)
    o_lat = _decode_attn(page_table, q_s, k_s, cache_ckv.reshape(cache_ckv.shape[1:]), kpe_t,
                         pages_per_chunk=16, pages_per_dot=8, n_slots=3)
    attn_s = _uv_proj(o_lat, w['wuv'], t_new=t_new)
    y_s = _out_mlp(xs, attn_s, gate_s, w, tm=tm_mlp, tf=tf)
    conv_s = utail_s.reshape(dec_batch, t_new, D_CONV)[:, t_new - (CONV_W - 1):]

    return (y_p.reshape(batch, seq, D_MODEL),
            y_s.reshape(dec_batch, t_new, D_MODEL),
            ckv_p.reshape(depth, batch, seq, KV_LORA),
            kpe_p.reshape(depth, batch, seq, ROPE_DIM),
            conv_p.reshape(depth, batch, CONV_W - 1, D_CONV),
            ckv_s.reshape(depth, dec_batch, t_new, KV_LORA),
            kpe_s.reshape(depth, dec_batch, t_new, ROPE_DIM),
            conv_s.reshape(depth, dec_batch, CONV_W - 1, D_CONV))
```

```python
import functools

import jax
import jax.numpy as jnp
from jax import lax
from jax.experimental import pallas as pl
from jax.experimental.pallas import tpu as pltpu

D_MODEL = 2048
N_HEADS = 8
Q_LORA = 512
KV_LORA = 512
NOPE_DIM = 128
ROPE_DIM = 64
V_DIM = 128
ROPE_BASE = 10000.0
D_ATTN = N_HEADS * V_DIM
D_CONV = D_MODEL - D_ATTN
CONV_W = 3
D_FF = 4 * D_MODEL
EPS = 1e-6
PAGE_SIZE = 128
SCALE = (NOPE_DIM + ROPE_DIM) ** -0.5

LANES = 128
SUBLANES = 8
ROPE_PAD = LANES
QK_DIM = KV_LORA + ROPE_PAD
Q_COLS = NOPE_DIM + 2 * ROPE_PAD
KV_COLS = KV_LORA + 2 * ROPE_PAD
VMEM_LIMIT = 58 * 1024 * 1024

F32 = jnp.float32
BF16 = jnp.bfloat16


def _dot(a, b):
    return jnp.dot(a, b, preferred_element_type=F32)


def _dot_t(a, b):
    return lax.dot_general(a, b, (((1,), (1,)), ((), ())), preferred_element_type=F32)


def _rms(x, g):
    r = lax.rsqrt(jnp.mean(x * x, axis=-1, keepdims=True) + EPS)
    return (x * r) * g


def _resident(shape):
    return pl.BlockSpec(shape, lambda *_: (0,) * len(shape), pipeline_mode=pl.Buffered(1))


def _store_transposed(dst_ref, latent, rope):
    n_tiles, _, tq = dst_ref.shape
    for t in range(n_tiles):
        rows = slice(t * tq, (t + 1) * tq)
        for j in range(KV_LORA // LANES):
            cols = slice(j * LANES, (j + 1) * LANES)
            dst_ref[t, cols, :] = latent[rows, cols].T.astype(BF16)
        dst_ref[t, KV_LORA:, :] = rope[rows].T.astype(BF16)


def _mix_kernel(seq_len, has_prev, conv_chunk, *refs):
    if has_prev:
        (x_ref, g1_ref, wq_ref, wkv_ref, wc_ref, qg_ref, wuq_ref, kvg_ref, wuk_ref, cw_ref,
         cos_ref, sin_ref, p1_ref, p2_ref,
         qcat_ref, kcat_ref, ckv_ref, kpe_ref, gate_ref, utail_ref, ubuf_ref) = refs
    else:
        (x_ref, g1_ref, wq_ref, wkv_ref, wc_ref, qg_ref, wuq_ref, kvg_ref, wuk_ref, cw_ref,
         cos_ref, sin_ref,
         qcat_ref, kcat_ref, ckv_ref, kpe_ref, gate_ref, utail_ref, kt_ref, ubuf_ref) = refs
    tm = x_ref.shape[0]
    tail = utail_ref.shape[-2]
    i = pl.program_id(0)

    @pl.when(i == 0)
    def _():
        ubuf_ref[0:SUBLANES, :] = jnp.zeros((SUBLANES, D_CONV), F32)

    xn = _rms(x_ref[...], g1_ref[...]).astype(BF16)
    cos = cos_ref[...]
    sin = sin_ref[...]

    zkv = _dot_t(xn, wkv_ref[...])
    ckv = _rms(zkv[:, :KV_LORA], kvg_ref[...])
    kpe = zkv[:, KV_LORA:KV_LORA + ROPE_PAD] * cos + zkv[:, KV_LORA + ROPE_PAD:] * sin
    ckv_ref[...] = ckv
    kpe_ref[...] = kpe[:, :ROPE_DIM]
    kcat_ref[:, :KV_LORA] = ckv.astype(BF16)
    kcat_ref[:, KV_LORA:] = kpe.astype(BF16)
    if not has_prev:
        _store_transposed(kt_ref, ckv, kpe)

    rq = _rms(_dot_t(xn, wq_ref[...]), qg_ref[...]).astype(BF16)
    for h in range(N_HEADS):
        qh = _dot(rq, wuq_ref[:, h * Q_COLS:(h + 1) * Q_COLS])
        q_abs = _dot(qh[:, :NOPE_DIM].astype(BF16), wuk_ref[h]) * SCALE
        q_pe = (qh[:, NOPE_DIM:NOPE_DIM + ROPE_PAD] * cos + qh[:, NOPE_DIM + ROPE_PAD:] * sin) * SCALE
        if has_prev:
            nq, _, tq, _ = qcat_ref.shape
            qcat_ref[:, h, :, 0:KV_LORA] = q_abs.reshape(nq, tq, KV_LORA)
            qcat_ref[:, h, :, KV_LORA:] = q_pe.reshape(nq, tq, ROPE_PAD)
        else:
            _store_transposed(qcat_ref.at[:, h], q_abs, q_pe)

    pos = (i * tm + lax.broadcasted_iota(jnp.int32, (tm, 1), 0)) % seq_len
    m1 = pos >= 1
    m2 = pos >= 2
    for j in range(D_CONV // conv_chunk):
        sl = slice(j * conv_chunk, (j + 1) * conv_chunk)
        b_g = _dot_t(xn, wc_ref[sl, :])
        c_g = _dot_t(xn, wc_ref[D_CONV + j * conv_chunk:D_CONV + (j + 1) * conv_chunk, :])
        x_in = _dot_t(xn, wc_ref[2 * D_CONV + j * conv_chunk:2 * D_CONV + (j + 1) * conv_chunk, :])
        u = c_g * x_in
        ubuf_ref[SUBLANES:SUBLANES + tm, sl] = u
        u1 = ubuf_ref[SUBLANES - 1:SUBLANES - 1 + tm, sl]
        u2 = ubuf_ref[SUBLANES - 2:SUBLANES - 2 + tm, sl]
        if has_prev:
            u1 = jnp.where(m1, u1, p1_ref[:, sl])
            u2 = jnp.where(m2, u2, p2_ref[:, sl])
        else:
            u1 = jnp.where(m1, u1, 0.0)
            u2 = jnp.where(m2, u2, 0.0)
        y = cw_ref[0:1, sl] * u2 + cw_ref[1:2, sl] * u1 + cw_ref[2:3, sl] * u
        gate_ref[:, sl] = (b_g * y).astype(BF16)
    utail_ref[0] = ubuf_ref[SUBLANES + tm - tail:SUBLANES + tm, :]
    ubuf_ref[0:SUBLANES, :] = ubuf_ref[tm:tm + SUBLANES, :]


def _mix_in(x, w, cos, sin, prev, *, seq_len, tm, tail, tq=None):
    t = x.shape[0]
    nt = t // tm
    has_prev = prev is not None
    tab_blocks = cos.shape[0] // tm
    row = lambda i: (i, 0)
    in_specs = [
        pl.BlockSpec((tm, D_MODEL), row),
        _resident((1, D_MODEL)),
        _resident((Q_LORA, D_MODEL)),
        _resident((KV_COLS, D_MODEL)),
        _resident((3 * D_CONV, D_MODEL)),
        _resident((1, Q_LORA)),
        _resident((Q_LORA, N_HEADS * Q_COLS)),
        _resident((1, KV_LORA)),
        _resident((N_HEADS, NOPE_DIM, KV_LORA)),
        _resident((CONV_W, D_CONV)),
        pl.BlockSpec((tm, ROPE_PAD), lambda i: (i % tab_blocks, 0)),
        pl.BlockSpec((tm, ROPE_PAD), lambda i: (i % tab_blocks, 0)),
    ]
    args = [x, w['g1'], w['wq_t'], w['wkv_t'], w['wc_t'], w['qg'], w['wuq'], w['kvg'], w['wuk'], w['cw'],
            cos, sin]
    if has_prev:
        in_specs += [pl.BlockSpec((tm, D_CONV), row), pl.BlockSpec((tm, D_CONV), row)]
        args += list(prev)
    if has_prev:
        q_shape = jax.ShapeDtypeStruct((t // seq_len, N_HEADS, seq_len, QK_DIM), F32)
        q_spec = pl.BlockSpec((tm // seq_len, N_HEADS, seq_len, QK_DIM), lambda i: (i, 0, 0, 0))
    else:
        q_shape = jax.ShapeDtypeStruct((t // tq, N_HEADS, QK_DIM, tq), BF16)
        q_spec = pl.BlockSpec((tm // tq, N_HEADS, QK_DIM, tq), lambda i: (i, 0, 0, 0))
    out_shape = [
        q_shape,
        jax.ShapeDtypeStruct((t, QK_DIM), BF16),
        jax.ShapeDtypeStruct((t, KV_LORA), F32),
        jax.ShapeDtypeStruct((t, ROPE_DIM), F32),
        jax.ShapeDtypeStruct((t, D_CONV), BF16),
        jax.ShapeDtypeStruct((nt, tail, D_CONV), F32),
    ]
    out_specs = [
        q_spec,
        pl.BlockSpec((tm, QK_DIM), row),
        pl.BlockSpec((tm, KV_LORA), row),
        pl.BlockSpec((tm, ROPE_DIM), row),
        pl.BlockSpec((tm, D_CONV), row),
        pl.BlockSpec((1, tail, D_CONV), lambda i: (i, 0, 0)),
    ]
    if not has_prev:
        out_shape.append(jax.ShapeDtypeStruct((t // tq, QK_DIM, tq), BF16))
        out_specs.append(pl.BlockSpec((tm // tq, QK_DIM, tq), lambda i: (i, 0, 0)))
    return pl.pallas_call(
        functools.partial(_mix_kernel, seq_len, has_prev, 256),
        grid=(nt,),
        in_specs=in_specs,
        out_specs=out_specs,
        out_shape=out_shape,
        scratch_shapes=[pltpu.VMEM((tm + SUBLANES, D_CONV), F32)],
        compiler_params=pltpu.CompilerParams(
            dimension_semantics=("arbitrary",), vmem_limit_bytes=VMEM_LIMIT),
        name="mix_in_prev" if has_prev else "mix_in",
    )(*args)


def _prompt_attn_kernel(qt_ref, kt_ref, k_ref, wuvt_ref, o_ref, m_ref, l_ref, acc_ref):
    tq = qt_ref.shape[-1]
    tk = k_ref.shape[1]
    qi = pl.program_id(1)

    def all_scores(k):
        return [_dot(k, qt_ref[0, h]) for h in range(N_HEADS)]

    k = k_ref[qi]
    v_t = kt_ref[qi, 0:KV_LORA, :]
    visible = (lax.broadcasted_iota(jnp.int32, (tk, tq), 0)
               <= lax.broadcasted_iota(jnp.int32, (tk, tq), 1))
    for h, s_t in enumerate(all_scores(k)):
        s_t = jnp.where(visible, s_t, -jnp.inf)
        m = jnp.max(s_t, axis=0, keepdims=True)
        p_t = jnp.exp(s_t - m)
        m_ref[h] = m
        l_ref[h] = jnp.sum(p_t, axis=0, keepdims=True)
        acc_ref[h] = _dot(v_t, p_t.astype(BF16))

    def update(k, v_t):
        for h, s_t in enumerate(all_scores(k)):
            m_old = m_ref[h]
            m_new = jnp.maximum(m_old, jnp.max(s_t, axis=0, keepdims=True))
            corr = jnp.exp(m_old - m_new)
            p_t = jnp.exp(s_t - m_new)
            l_ref[h] = l_ref[h] * corr + jnp.sum(p_t, axis=0, keepdims=True)
            acc_ref[h] = acc_ref[h] * corr + _dot(v_t, p_t.astype(BF16))
            m_ref[h] = m_new

    def pair_step(i, carry):
        k2 = k_ref[pl.ds(2 * i, 2)].reshape(2 * tk, QK_DIM)
        v_t2 = jnp.concatenate([kt_ref[2 * i, 0:KV_LORA, :], kt_ref[2 * i + 1, 0:KV_LORA, :]], axis=1)
        update(k2, v_t2)
        return carry

    lax.fori_loop(0, qi // 2, pair_step, 0)

    @pl.when(qi % 2 == 1)
    def _():
        update(k_ref[qi - 1], kt_ref[qi - 1, 0:KV_LORA, :])

    for h in range(N_HEADS):
        o_t = (acc_ref[h] * (1.0 / l_ref[h])).astype(BF16)
        a_t = _dot(wuvt_ref[h], o_t)
        o_ref[:, h * V_DIM:(h + 1) * V_DIM] = a_t.T.astype(o_ref.dtype)


def _prompt_attn(q_t, kcat_t, kcat, wuv_t, *, batch, seq, tq):
    nq = seq // tq
    return pl.pallas_call(
        _prompt_attn_kernel,
        grid=(batch, nq),
        in_specs=[
            pl.BlockSpec((1, N_HEADS, QK_DIM, tq), lambda b, i: (b * nq + i, 0, 0, 0)),
            pl.BlockSpec((nq, QK_DIM, tq), lambda b, i: (b, 0, 0)),
            pl.BlockSpec((nq, tq, QK_DIM), lambda b, i: (b, 0, 0)),
            _resident((N_HEADS, V_DIM, KV_LORA)),
        ],
        out_specs=pl.BlockSpec((tq, D_ATTN), lambda b, i: (b * nq + i, 0)),
        out_shape=jax.ShapeDtypeStruct((batch * seq, D_ATTN), BF16),
        scratch_shapes=[pltpu.VMEM((N_HEADS, 1, tq), F32), pltpu.VMEM((N_HEADS, 1, tq), F32),
                        pltpu.VMEM((N_HEADS, KV_LORA, tq), F32)],
        compiler_params=pltpu.CompilerParams(
            dimension_semantics=("arbitrary", "arbitrary"), vmem_limit_bytes=VMEM_LIMIT),
        name="prompt_attn",
    )(q_t, kcat_t, kcat.reshape(batch * nq, tq, QK_DIM), wuv_t)


def _decode_attn_kernel(pages_per_dot, pt_ref, q_ref, kn_ref, ckv_hbm, kpet_hbm,
                        o_ref, kbuf, pbuf, sem):
    n_seq, n_pages = pt_ref.shape
    n_slots, pages_per_chunk = kbuf.shape[0], kbuf.shape[1]
    n_chunks = n_pages // pages_per_chunk
    ahead = n_slots - 1
    dot_keys = pages_per_dot * PAGE_SIZE
    t_new = kn_ref.shape[1]
    rows = N_HEADS * t_new
    b = pl.program_id(0)

    def copies(g):
        seq, chunk, slot = g // n_chunks, g % n_chunks, g % n_slots
        out = []
        for p in range(pages_per_chunk):
            page = pt_ref[seq, chunk * pages_per_chunk + p]
            out.append(pltpu.make_async_copy(ckv_hbm.at[page], kbuf.at[slot, p], sem.at[slot]))
            out.append(pltpu.make_async_copy(kpet_hbm.at[page], pbuf.at[slot, p], sem.at[slot]))
        return out

    @pl.when(b == 0)
    def _():
        for g in range(ahead):
            for c in copies(g):
                c.start()

    q = q_ref[0].reshape(rows, QK_DIM).astype(BF16)
    qa = q[:, :KV_LORA]
    qp = q[:, KV_LORA:KV_LORA + ROPE_DIM]

    kn = kn_ref[0]
    s = _dot_t(q, kn)
    t_row = lax.broadcasted_iota(jnp.int32, (rows, t_new), 0) % t_new
    t_col = lax.broadcasted_iota(jnp.int32, (rows, t_new), 1)
    s = jnp.where(t_col <= t_row, s, -jnp.inf)
    m0 = jnp.max(s, axis=-1, keepdims=True)
    p0 = jnp.exp(s - m0)
    init = (m0, jnp.sum(p0, axis=-1, keepdims=True), _dot(p0.astype(BF16), kn[:, :KV_LORA]))

    def chunk_step(c, carry):
        m, l, acc = carry
        g = b * n_chunks + c
        slot = g % n_slots

        @pl.when(g + ahead < n_seq * n_chunks)
        def _():
            for cp in copies(g + ahead):
                cp.start()

        for cp in copies(g):
            cp.wait()
        keys, scores = [], []
        for j in range(pages_per_chunk // pages_per_dot):
            pages = range(j * pages_per_dot, (j + 1) * pages_per_dot)
            k = kbuf[slot, j * pages_per_dot:(j + 1) * pages_per_dot].reshape(dot_keys, KV_LORA)
            k = k.astype(BF16)
            kp_t = jnp.concatenate([pbuf[slot, p] for p in pages], axis=1).astype(BF16)
            keys.append(k)
            scores.append(_dot_t(qa, k) + _dot(qp, kp_t))
        for k, s in zip(keys, scores):
            m_new = jnp.maximum(m, jnp.max(s, axis=-1, keepdims=True))
            corr = jnp.exp(m - m_new)
            p = jnp.exp(s - m_new)
            l = l * corr + jnp.sum(p, axis=-1, keepdims=True)
            acc = acc * corr + _dot(p.astype(BF16), k)
            m = m_new
        return m, l, acc

    _, l, acc = lax.fori_loop(0, n_chunks, chunk_step, init)
    o_ref[0] = acc * (1.0 / l)


def _decode_attn(page_table, qcat, kcat, cache_ckv, cache_kpe_t, *, pages_per_chunk, pages_per_dot,
                 n_slots):
    n_seq, n_pages = page_table.shape
    t_new = qcat.shape[2]
    rows = N_HEADS * t_new
    assert n_pages % pages_per_chunk == 0 and pages_per_chunk % pages_per_dot == 0
    assert n_slots - 1 <= n_seq * (n_pages // pages_per_chunk)
    grid_spec = pltpu.PrefetchScalarGridSpec(
        num_scalar_prefetch=1,
        grid=(n_seq,),
        in_specs=[
            pl.BlockSpec((1, N_HEADS, t_new, QK_DIM), lambda b, pt: (b, 0, 0, 0)),
            pl.BlockSpec((1, t_new, QK_DIM), lambda b, pt: (b, 0, 0)),
            pl.BlockSpec(memory_space=pl.ANY),
            pl.BlockSpec(memory_space=pl.ANY),
        ],
        out_specs=pl.BlockSpec((1, rows, KV_LORA), lambda b, pt: (b, 0, 0)),
        scratch_shapes=[
            pltpu.VMEM((n_slots, pages_per_chunk, PAGE_SIZE, KV_LORA), F32),
            pltpu.VMEM((n_slots, pages_per_chunk, ROPE_DIM, PAGE_SIZE), F32),
            pltpu.SemaphoreType.DMA((n_slots,)),
        ],
    )
    return pl.pallas_call(
        functools.partial(_decode_attn_kernel, pages_per_dot),
        grid_spec=grid_spec,
        out_shape=jax.ShapeDtypeStruct((n_seq, rows, KV_LORA), F32),
        compiler_params=pltpu.CompilerParams(
            dimension_semantics=("arbitrary",), vmem_limit_bytes=VMEM_LIMIT),
        name="decode_attn",
    )(page_table, qcat, kcat.reshape(n_seq, t_new, QK_DIM), cache_ckv, cache_kpe_t)


def _uv_proj_kernel(o_ref, wuv_ref, a_ref):
    n_seq, _, t_new, _ = o_ref.shape
    o = o_ref[:, 0].reshape(n_seq * t_new, KV_LORA).astype(BF16)
    a_ref[...] = _dot(o, wuv_ref[0])


def _uv_proj(o_lat, wuv, *, t_new):
    n_seq = o_lat.shape[0]
    return pl.pallas_call(
        _uv_proj_kernel,
        grid=(N_HEADS,),
        in_specs=[
            pl.BlockSpec((n_seq, 1, t_new, KV_LORA), lambda h: (0, h, 0, 0)),
            pl.BlockSpec((1, KV_LORA, V_DIM), lambda h: (h, 0, 0)),
        ],
        out_specs=pl.BlockSpec((n_seq * t_new, V_DIM), lambda h: (0, h)),
        out_shape=jax.ShapeDtypeStruct((n_seq * t_new, D_ATTN), F32),
        compiler_params=pltpu.CompilerParams(dimension_semantics=("arbitrary",)),
        name="uv_proj",
    )(o_lat.reshape(n_seq, N_HEADS, t_new, KV_LORA), wuv)


def _out_mlp_kernel(x_ref, attn_ref, gate_ref, wo_ref, g2_ref, wup_ref, wdn_ref, fg_ref,
                    o_ref, hn_ref):
    f = pl.program_id(1)

    @pl.when(f == 0)
    def _():
        mix_o = (_dot(attn_ref[...].astype(BF16), wo_ref[0:D_ATTN, :])
                 + _dot(gate_ref[...], wo_ref[D_ATTN:, :]))
        h = x_ref[...] + mix_o
        o_ref[...] = h
        hn_ref[...] = _rms(h, g2_ref[...]).astype(BF16)

    a = _dot(hn_ref[...], wup_ref[...])
    a = jnp.square(jnp.maximum(a, 0.0)).astype(BF16)
    o_ref[...] += _dot(a, wdn_ref[...])

    @pl.when(f == pl.num_programs(1) - 1)
    def _():
        o_ref[...] = _rms(o_ref[...], fg_ref[...])


def _out_mlp(x, attn, gate, w, *, tm, tf):
    t = x.shape[0]
    row = lambda i, f: (i, 0)
    return pl.pallas_call(
        _out_mlp_kernel,
        grid=(t // tm, D_FF // tf),
        in_specs=[
            pl.BlockSpec((tm, D_MODEL), row),
            pl.BlockSpec((tm, D_ATTN), row),
            pl.BlockSpec((tm, D_CONV), row),
            _resident((D_MODEL, D_MODEL)),
            _resident((1, D_MODEL)),
            pl.BlockSpec((D_MODEL, tf), lambda i, f: (0, f)),
            pl.BlockSpec((tf, D_MODEL), lambda i, f: (f, 0)),
            _resident((1, D_MODEL)),
        ],
        out_specs=pl.BlockSpec((tm, D_MODEL), row),
        out_shape=jax.ShapeDtypeStruct((t, D_MODEL), F32),
        scratch_shapes=[pltpu.VMEM((tm, D_MODEL), BF16)],
        compiler_params=pltpu.CompilerParams(
            dimension_semantics=("arbitrary", "arbitrary"), vmem_limit_bytes=VMEM_LIMIT),
        name="out_mlp",
    )(x, attn, gate, w['wo'], w['g2'], w['wup'], w['wdn'], w['fg'])


def _pad_lanes(a):
    return jnp.concatenate([a, jnp.zeros(a.shape[:-1] + (ROPE_PAD - a.shape[-1],), a.dtype)], -1)


def _rot_cols(a):
    half = ROPE_DIM // 2
    return jnp.concatenate([-a[..., half:], a[..., :half]], -1)


def _layer_weights(norm1_g, w_in, q_norm_g, w_uq, kv_norm_g, w_uk, w_uv, conv_w, w_o, norm2_g,
                   w_up, w_down, final_g):
    o1 = Q_LORA
    o2 = o1 + KV_LORA
    o3 = o2 + ROPE_DIM
    w_in_t = w_in.T
    w_kpe_t = w_in_t[o2:o3]
    wkv_t = jnp.concatenate([w_in_t[o1:o2], _pad_lanes(w_kpe_t.T).T, _pad_lanes(_rot_cols(w_kpe_t.T)).T], 0)
    wuq = w_uq.reshape(Q_LORA, N_HEADS, NOPE_DIM + ROPE_DIM)
    wuq_pe = wuq[..., NOPE_DIM:]
    wuq = jnp.concatenate([wuq[..., :NOPE_DIM], _pad_lanes(wuq_pe), _pad_lanes(_rot_cols(wuq_pe))], -1)
    return {
        'g1': norm1_g.reshape(1, D_MODEL),
        'wq_t': w_in_t[:o1].astype(BF16),
        'wkv_t': wkv_t.astype(BF16),
        'wc_t': w_in_t[o3:].astype(BF16),
        'qg': q_norm_g.reshape(1, Q_LORA),
        'wuq': wuq.reshape(Q_LORA, N_HEADS * Q_COLS).astype(BF16),
        'kvg': kv_norm_g.reshape(1, KV_LORA),
        'wuk': w_uk.transpose(1, 2, 0).astype(BF16),
        'wuv': w_uv.transpose(1, 0, 2).astype(BF16),
        'wuv_t': w_uv.transpose(1, 2, 0).astype(BF16),
        'cw': conv_w,
        'wo': w_o.astype(BF16),
        'g2': norm2_g.reshape(1, D_MODEL),
        'wup': w_up.astype(BF16),
        'wdn': w_down.astype(BF16),
        'fg': final_g.reshape(1, D_MODEL),
    }


def _rope_tables(pos):
    inv = ROPE_BASE ** (-jnp.arange(0, ROPE_DIM, 2, dtype=F32) / ROPE_DIM)
    ang = pos.astype(F32)[:, None] * inv[None, :]
    cos, sin = jnp.cos(ang), jnp.sin(ang)
    return _pad_lanes(jnp.concatenate([cos, cos], -1)), _pad_lanes(jnp.concatenate([sin, sin], -1))


def kernel(x_prompt, x_sample, cache_ckv, cache_kpe, state_conv, page_table, norm1_g, w_in,
           q_norm_g, w_uq, kv_norm_g, w_uk, w_uv, conv_w, w_o, norm2_g, w_up, w_down, final_g):
    batch, seq, _ = x_prompt.shape
    dec_batch, t_new, _ = x_sample.shape
    depth = norm1_g.shape[0]
    assert depth == 1
    past = page_table.shape[1] * PAGE_SIZE
    tm_mix_p, tm_mix_s, tq, tm_mlp, tf = 512, 256, 256, 512, 1024

    w = _layer_weights(norm1_g[0], w_in[0], q_norm_g[0], w_uq[0], kv_norm_g[0], w_uk[0], w_uv[0],
                       conv_w[0], w_o[0], norm2_g[0], w_up[0], w_down[0], final_g)

    xp = x_prompt.reshape(batch * seq, D_MODEL)
    cos_p, sin_p = _rope_tables(jnp.arange(seq, dtype=jnp.int32))
    qt_p, k_p, ckv_p, kpe_p, gate_p, utail_p, kt_p = _mix_in(
        xp, w, cos_p, sin_p, None, seq_len=seq, tm=tm_mix_p, tail=SUBLANES, tq=tq)
    attn_p = _prompt_attn(qt_p, kt_p, k_p, w['wuv_t'], batch=batch, seq=seq, tq=tq)
    y_p = _out_mlp(xp, attn_p, gate_p, w, tm=tm_mlp, tf=tf)
    conv_p = utail_p.reshape(batch, seq // tm_mix_p, SUBLANES, D_CONV)[:, -1, SUBLANES - (CONV_W - 1):]

    xs = x_sample.reshape(dec_batch * t_new, D_MODEL)
    cos_s, sin_s = _rope_tables(past + jnp.arange(t_new, dtype=jnp.int32))
    reps = tm_mix_s // t_new
    cos_s, sin_s = jnp.tile(cos_s, (reps, 1)), jnp.tile(sin_s, (reps, 1))
    st = state_conv[0]
    zeros = jnp.zeros((dec_batch, t_new, D_CONV), F32)
    prev1 = zeros.at[:, 0].set(st[:, 1]).reshape(dec_batch * t_new, D_CONV)
    prev2 = zeros.at[:, 0].set(st[:, 0]).at[:, 1].set(st[:, 1]).reshape(dec_batch * t_new, D_CONV)
    q_s, k_s, ckv_s, kpe_s, gate_s, utail_s = _mix_in(
        xs, w, cos_s, sin_s, (prev1, prev2), seq_len=t_new, tm=tm_mix_s, tail=tm_mix_s)
    kpe_t = jnp.swapaxes(cache_kpe.reshape(cache_kpe.shape[1:]), 1, 2)
    o_lat = _decode_attn(page_table, q_s, k_s, cache_ckv.reshape(cache_ckv.shape[1:]), kpe_t,
                         pages_per_chunk=16, pages_per_dot=8, n_slots=4)
    attn_s = _uv_proj(o_lat, w['wuv'], t_new=t_new)
    y_s = _out_mlp(xs, attn_s, gate_s, w, tm=tm_mlp, tf=tf)
    conv_s = utail_s.reshape(dec_batch, t_new, D_CONV)[:, t_new - (CONV_W - 1):]

    return (y_p.reshape(batch, seq, D_MODEL),
            y_s.reshape(dec_batch, t_new, D_MODEL),
            ckv_p.reshape(depth, batch, seq, KV_LORA),
            kpe_p.reshape(depth, batch, seq, ROPE_DIM),
            conv_p.reshape(depth, batch, CONV_W - 1, D_CONV),
            ckv_s.reshape(depth, dec_batch, t_new, KV_LORA),
            kpe_s.reshape(depth, dec_batch, t_new, ROPE_DIM),
            conv_s.reshape(depth, dec_batch, CONV_W - 1, D_CONV))
```

```python
import functools

import jax
import jax.numpy as jnp
from jax import lax
from jax.experimental import pallas as pl
from jax.experimental.pallas import tpu as pltpu

D_MODEL = 2048
N_HEADS = 8
Q_LORA = 512
KV_LORA = 512
NOPE_DIM = 128
ROPE_DIM = 64
V_DIM = 128
ROPE_BASE = 10000.0
D_ATTN = N_HEADS * V_DIM
D_CONV = D_MODEL - D_ATTN
CONV_W = 3
D_FF = 4 * D_MODEL
EPS = 1e-6
PAGE_SIZE = 128
SCALE = (NOPE_DIM + ROPE_DIM) ** -0.5

LANES = 128
SUBLANES = 8
ROPE_PAD = LANES
assert 2 * ROPE_DIM == ROPE_PAD
QK_DIM = KV_LORA + ROPE_PAD
Q_COLS = NOPE_DIM + ROPE_PAD
KV_COLS = KV_LORA + ROPE_PAD
VMEM_LIMIT = 58 * 1024 * 1024

F32 = jnp.float32
BF16 = jnp.bfloat16


def _dot(a, b):
    return jnp.dot(a, b, preferred_element_type=F32)


def _dot_t(a, b):
    return lax.dot_general(a, b, (((1,), (1,)), ((), ())), preferred_element_type=F32)


def _rms(x, g):
    r = lax.rsqrt(jnp.mean(x * x, axis=-1, keepdims=True) + EPS)
    return (x * r) * g


def _rope(xr, cs):
    t = xr * cs
    return t + pltpu.roll(t, ROPE_DIM, axis=1)


def _resident(shape):
    return pl.BlockSpec(shape, lambda *_: (0,) * len(shape), pipeline_mode=pl.Buffered(1))


def _store_transposed(dst_ref, latent, rope):
    n_tiles, _, tq = dst_ref.shape
    for t in range(n_tiles):
        rows = slice(t * tq, (t + 1) * tq)
        for j in range(KV_LORA // LANES):
            cols = slice(j * LANES, (j + 1) * LANES)
            dst_ref[t, cols, :] = latent[rows, cols].T.astype(BF16)
        dst_ref[t, KV_LORA:, :] = rope[rows].T.astype(BF16)


def _mix_kernel(seq_len, has_prev, conv_chunk, *refs):
    if has_prev:
        (x_ref, g1_ref, wq_ref, wkv_ref, wc_ref, qg_ref, wuq_ref, kvg_ref, wuk_ref, cw_ref,
         cs_ref, p1_ref, p2_ref,
         qcat_ref, kcat_ref, ckv_ref, kpe_ref, gate_ref, utail_ref, ubuf_ref) = refs
    else:
        (x_ref, g1_ref, wq_ref, wkv_ref, wc_ref, qg_ref, wuq_ref, kvg_ref, wuk_ref, cw_ref,
         cs_ref,
         qcat_ref, kcat_ref, ckv_ref, kpe_ref, gate_ref, utail_ref, kt_ref, ubuf_ref) = refs
    tm = x_ref.shape[0]
    tail = utail_ref.shape[-2]
    i = pl.program_id(0)

    @pl.when(i == 0)
    def _():
        ubuf_ref[0:SUBLANES, :] = jnp.zeros((SUBLANES, D_CONV), F32)

    xn = _rms(x_ref[...], g1_ref[...]).astype(BF16)
    cs = cs_ref[...]

    zkv = _dot_t(xn, wkv_ref[...])
    ckv = _rms(zkv[:, :KV_LORA], kvg_ref[...])
    lane = lax.broadcasted_iota(jnp.int32, (tm, ROPE_PAD), 1)
    kpe = jnp.where(lane < ROPE_DIM, _rope(zkv[:, KV_LORA:], cs), 0.0)
    ckv_ref[...] = ckv
    kpe_ref[...] = kpe[:, :ROPE_DIM]
    kcat_ref[:, :KV_LORA] = ckv.astype(BF16)
    kcat_ref[:, KV_LORA:] = kpe.astype(BF16)
    if not has_prev:
        _store_transposed(kt_ref, ckv, kpe)

    rq = _rms(_dot_t(xn, wq_ref[...]), qg_ref[...]).astype(BF16)
    for h in range(N_HEADS):
        qh = _dot(rq, wuq_ref[:, h * Q_COLS:(h + 1) * Q_COLS])
        q_abs = _dot(qh[:, :NOPE_DIM].astype(BF16), wuk_ref[h]) * SCALE
        q_pe = _rope(qh[:, NOPE_DIM:], cs) * SCALE
        if has_prev:
            nq, _, tq, _ = qcat_ref.shape
            qcat_ref[:, h, :, 0:KV_LORA] = q_abs.reshape(nq, tq, KV_LORA)
            qcat_ref[:, h, :, KV_LORA:] = q_pe.reshape(nq, tq, ROPE_PAD)
        else:
            _store_transposed(qcat_ref.at[:, h], q_abs, q_pe)

    pos = (i * tm + lax.broadcasted_iota(jnp.int32, (tm, 1), 0)) % seq_len
    m1 = pos >= 1
    m2 = pos >= 2
    for j in range(D_CONV // conv_chunk):
        sl = slice(j * conv_chunk, (j + 1) * conv_chunk)
        b_g = _dot_t(xn, wc_ref[sl, :])
        c_g = _dot_t(xn, wc_ref[D_CONV + j * conv_chunk:D_CONV + (j + 1) * conv_chunk, :])
        x_in = _dot_t(xn, wc_ref[2 * D_CONV + j * conv_chunk:2 * D_CONV + (j + 1) * conv_chunk, :])
        u = c_g * x_in
        ubuf_ref[SUBLANES:SUBLANES + tm, sl] = u
        u1 = ubuf_ref[SUBLANES - 1:SUBLANES - 1 + tm, sl]
        u2 = ubuf_ref[SUBLANES - 2:SUBLANES - 2 + tm, sl]
        if has_prev:
            u1 = jnp.where(m1, u1, p1_ref[:, sl])
            u2 = jnp.where(m2, u2, p2_ref[:, sl])
        else:
            u1 = jnp.where(m1, u1, 0.0)
            u2 = jnp.where(m2, u2, 0.0)
        y = cw_ref[0:1, sl] * u2 + cw_ref[1:2, sl] * u1 + cw_ref[2:3, sl] * u
        gate_ref[:, sl] = (b_g * y).astype(BF16)
    utail_ref[0] = ubuf_ref[SUBLANES + tm - tail:SUBLANES + tm, :]
    ubuf_ref[0:SUBLANES, :] = ubuf_ref[tm:tm + SUBLANES, :]


def _mix_in(x, w, cs, prev, *, seq_len, tm, tail, tq=None):
    t = x.shape[0]
    nt = t // tm
    has_prev = prev is not None
    tab_blocks = cs.shape[0] // tm
    row = lambda i: (i, 0)
    in_specs = [
        pl.BlockSpec((tm, D_MODEL), row),
        _resident((1, D_MODEL)),
        _resident((Q_LORA, D_MODEL)),
        _resident((KV_COLS, D_MODEL)),
        _resident((3 * D_CONV, D_MODEL)),
        _resident((1, Q_LORA)),
        _resident((Q_LORA, N_HEADS * Q_COLS)),
        _resident((1, KV_LORA)),
        _resident((N_HEADS, NOPE_DIM, KV_LORA)),
        _resident((CONV_W, D_CONV)),
        pl.BlockSpec((tm, ROPE_PAD), lambda i: (i % tab_blocks, 0)),
    ]
    args = [x, w['g1'], w['wq_t'], w['wkv_t'], w['wc_t'], w['qg'], w['wuq'], w['kvg'], w['wuk'], w['cw'],
            cs]
    if has_prev:
        in_specs += [pl.BlockSpec((tm, D_CONV), row), pl.BlockSpec((tm, D_CONV), row)]
        args += list(prev)
    if has_prev:
        q_shape = jax.ShapeDtypeStruct((t // seq_len, N_HEADS, seq_len, QK_DIM), F32)
        q_spec = pl.BlockSpec((tm // seq_len, N_HEADS, seq_len, QK_DIM), lambda i: (i, 0, 0, 0))
    else:
        q_shape = jax.ShapeDtypeStruct((t // tq, N_HEADS, QK_DIM, tq), BF16)
        q_spec = pl.BlockSpec((tm // tq, N_HEADS, QK_DIM, tq), lambda i: (i, 0, 0, 0))
    out_shape = [
        q_shape,
        jax.ShapeDtypeStruct((t, QK_DIM), BF16),
        jax.ShapeDtypeStruct((t, KV_LORA), F32),
        jax.ShapeDtypeStruct((t, ROPE_DIM), F32),
        jax.ShapeDtypeStruct((t, D_CONV), BF16),
        jax.ShapeDtypeStruct((nt, tail, D_CONV), F32),
    ]
    out_specs = [
        q_spec,
        pl.BlockSpec((tm, QK_DIM), row),
        pl.BlockSpec((tm, KV_LORA), row),
        pl.BlockSpec((tm, ROPE_DIM), row),
        pl.BlockSpec((tm, D_CONV), row),
        pl.BlockSpec((1, tail, D_CONV), lambda i: (i, 0, 0)),
    ]
    if not has_prev:
        out_shape.append(jax.ShapeDtypeStruct((t // tq, QK_DIM, tq), BF16))
        out_specs.append(pl.BlockSpec((tm // tq, QK_DIM, tq), lambda i: (i, 0, 0)))
    return pl.pallas_call(
        functools.partial(_mix_kernel, seq_len, has_prev, 256),
        grid=(nt,),
        in_specs=in_specs,
        out_specs=out_specs,
        out_shape=out_shape,
        scratch_shapes=[pltpu.VMEM((tm + SUBLANES, D_CONV), F32)],
        compiler_params=pltpu.CompilerParams(
            dimension_semantics=("arbitrary",), vmem_limit_bytes=VMEM_LIMIT),
        name="mix_in_prev" if has_prev else "mix_in",
    )(*args)


def _prompt_attn_kernel(qt_ref, kt_ref, k_ref, wuvt_ref, wo_f32, wup_f32, wdn_f32,
                        o_ref, wo_bf16, wup_bf16, wdn_bf16, m_ref, l_ref, acc_ref):
    tq = qt_ref.shape[-1]
    tk = k_ref.shape[1]
    qi = pl.program_id(1)

    wo_bf16[...] = wo_f32[...].astype(BF16)
    wup_bf16[...] = wup_f32[...].astype(BF16)
    wdn_bf16[...] = wdn_f32[...].astype(BF16)

    def all_scores(k):
        return [_dot(k, qt_ref[0, h]) for h in range(N_HEADS)]

    k = k_ref[qi]
    v_t = kt_ref[qi, 0:KV_LORA, :]
    visible = (lax.broadcasted_iota(jnp.int32, (tk, tq), 0)
               <= lax.broadcasted_iota(jnp.int32, (tk, tq), 1))
    for h, s_t in enumerate(all_scores(k)):
        s_t = jnp.where(visible, s_t, -jnp.inf)
        m = jnp.max(s_t, axis=0, keepdims=True)
        p_t = jnp.exp(s_t - m)
        m_ref[h] = m
        l_ref[h] = jnp.sum(p_t, axis=0, keepdims=True)
        acc_ref[h] = _dot(v_t, p_t.astype(BF16))

    def update(k, v_t):
        for h, s_t in enumerate(all_scores(k)):
            m_old = m_ref[h]
            m_new = jnp.maximum(m_old, jnp.max(s_t, axis=0, keepdims=True))
            corr = jnp.exp(m_old - m_new)
            p_t = jnp.exp(s_t - m_new)
            l_ref[h] = l_ref[h] * corr + jnp.sum(p_t, axis=0, keepdims=True)
            acc_ref[h] = acc_ref[h] * corr + _dot(v_t, p_t.astype(BF16))
            m_ref[h] = m_new

    def pair_step(i, carry):
        k2 = k_ref[pl.ds(2 * i, 2)].reshape(2 * tk, QK_DIM)
        v_t2 = jnp.concatenate([kt_ref[2 * i, 0:KV_LORA, :], kt_ref[2 * i + 1, 0:KV_LORA, :]], axis=1)
        update(k2, v_t2)
        return carry

    lax.fori_loop(0, qi // 2, pair_step, 0)

    @pl.when(qi % 2 == 1)
    def _():
        update(k_ref[qi - 1], kt_ref[qi - 1, 0:KV_LORA, :])

    for h in range(N_HEADS):
        o_t = (acc_ref[h] * (1.0 / l_ref[h])).astype(BF16)
        a_t = _dot(wuvt_ref[h], o_t)
        o_ref[:, h * V_DIM:(h + 1) * V_DIM] = a_t.T.astype(o_ref.dtype)


def _prompt_attn(q_t, kcat_t, kcat, wuv_t, w_o, w_up, w_down, *, batch, seq, tq):
    nq = seq // tq
    steps = batch * nq
    step_rows = lambda b, i: (b * nq + i, 0)
    weights = (w_o, w_up, w_down)
    slabs = [(wt.shape[0] // steps, wt.shape[1]) for wt in weights]
    return pl.pallas_call(
        _prompt_attn_kernel,
        grid=(batch, nq),
        in_specs=[
            pl.BlockSpec((1, N_HEADS, QK_DIM, tq), lambda b, i: (b * nq + i, 0, 0, 0)),
            pl.BlockSpec((nq, QK_DIM, tq), lambda b, i: (b, 0, 0)),
            pl.BlockSpec((nq, tq, QK_DIM), lambda b, i: (b, 0, 0)),
            _resident((N_HEADS, V_DIM, KV_LORA)),
        ] + [pl.BlockSpec(slab, step_rows) for slab in slabs],
        out_specs=[pl.BlockSpec((tq, D_ATTN), step_rows)] + [pl.BlockSpec(slab, step_rows) for slab in slabs],
        out_shape=[jax.ShapeDtypeStruct((batch * seq, D_ATTN), BF16)]
        + [jax.ShapeDtypeStruct(wt.shape, BF16) for wt in weights],
        scratch_shapes=[pltpu.VMEM((N_HEADS, 1, tq), F32), pltpu.VMEM((N_HEADS, 1, tq), F32),
                        pltpu.VMEM((N_HEADS, KV_LORA, tq), F32)],
        compiler_params=pltpu.CompilerParams(
            dimension_semantics=("arbitrary", "arbitrary"), vmem_limit_bytes=VMEM_LIMIT),
        name="prompt_attn",
    )(q_t, kcat_t, kcat.reshape(batch * nq, tq, QK_DIM), wuv_t, *weights)


def _decode_attn_kernel(pages_per_dot, pt_ref, q_ref, kn_ref, ckv_hbm, kpet_hbm,
                        o_ref, kbuf, pbuf, sem):
    n_seq, n_pages = pt_ref.shape
    n_slots, pages_per_chunk = kbuf.shape[0], kbuf.shape[1]
    n_chunks = n_pages // pages_per_chunk
    ahead = n_slots - 1
    dot_keys = pages_per_dot * PAGE_SIZE
    t_new = kn_ref.shape[1]
    rows = N_HEADS * t_new
    b = pl.program_id(0)

    def copies(g):
        seq, chunk, slot = g // n_chunks, g % n_chunks, g % n_slots
        out = []
        for p in range(pages_per_chunk):
            page = pt_ref[seq, chunk * pages_per_chunk + p]
            out.append(pltpu.make_async_copy(ckv_hbm.at[page], kbuf.at[slot, p], sem.at[slot]))
            out.append(pltpu.make_async_copy(kpet_hbm.at[page], pbuf.at[slot, p], sem.at[slot]))
        return out

    @pl.when(b == 0)
    def _():
        for g in range(ahead):
            for c in copies(g):
                c.start()

    q = q_ref[0].reshape(rows, QK_DIM).astype(BF16)
    qa = q[:, :KV_LORA]
    qp = q[:, KV_LORA:KV_LORA + ROPE_DIM]

    kn = kn_ref[0]
    s = _dot_t(q, kn)
    t_row = lax.broadcasted_iota(jnp.int32, (rows, t_new), 0) % t_new
    t_col = lax.broadcasted_iota(jnp.int32, (rows, t_new), 1)
    s = jnp.where(t_col <= t_row, s, -jnp.inf)
    m0 = jnp.max(s, axis=-1, keepdims=True)
    p0 = jnp.exp(s - m0)
    init = (m0, jnp.sum(p0, axis=-1, keepdims=True), _dot(p0.astype(BF16), kn[:, :KV_LORA]))

    def chunk_step(c, carry):
        m, l, acc = carry
        g = b * n_chunks + c
        slot = g % n_slots

        @pl.when(g + ahead < n_seq * n_chunks)
        def _():
            for cp in copies(g + ahead):
                cp.start()

        for cp in copies(g):
            cp.wait()
        keys, scores = [], []
        for j in range(pages_per_chunk // pages_per_dot):
            pages = range(j * pages_per_dot, (j + 1) * pages_per_dot)
            k = kbuf[slot, j * pages_per_dot:(j + 1) * pages_per_dot].reshape(dot_keys, KV_LORA)
            k = k.astype(BF16)
            kp_t = jnp.concatenate([pbuf[slot, p] for p in pages], axis=1).astype(BF16)
            keys.append(k)
            scores.append(_dot_t(qa, k) + _dot(qp, kp_t))
        for k, s in zip(keys, scores):
            m_new = jnp.maximum(m, jnp.max(s, axis=-1, keepdims=True))
            corr = jnp.exp(m - m_new)
            p = jnp.exp(s - m_new)
            l = l * corr + jnp.sum(p, axis=-1, keepdims=True)
            acc = acc * corr + _dot(p.astype(BF16), k)
            m = m_new
        return m, l, acc

    _, l, acc = lax.fori_loop(0, n_chunks, chunk_step, init)
    o_ref[0] = acc * (1.0 / l)


def _decode_attn(page_table, qcat, kcat, cache_ckv, cache_kpe_t, *, pages_per_chunk, pages_per_dot,
                 n_slots):
    n_seq, n_pages = page_table.shape
    t_new = qcat.shape[2]
    rows = N_HEADS * t_new
    assert n_pages % pages_per_chunk == 0 and pages_per_chunk % pages_per_dot == 0
    assert n_slots - 1 <= n_seq * (n_pages // pages_per_chunk)
    grid_spec = pltpu.PrefetchScalarGridSpec(
        num_scalar_prefetch=1,
        grid=(n_seq,),
        in_specs=[
            pl.BlockSpec((1, N_HEADS, t_new, QK_DIM), lambda b, pt: (b, 0, 0, 0)),
            pl.BlockSpec((1, t_new, QK_DIM), lambda b, pt: (b, 0, 0)),
            pl.BlockSpec(memory_space=pl.ANY),
            pl.BlockSpec(memory_space=pl.ANY),
        ],
        out_specs=pl.BlockSpec((1, rows, KV_LORA), lambda b, pt: (b, 0, 0)),
        scratch_shapes=[
            pltpu.VMEM((n_slots, pages_per_chunk, PAGE_SIZE, KV_LORA), F32),
            pltpu.VMEM((n_slots, pages_per_chunk, ROPE_DIM, PAGE_SIZE), F32),
            pltpu.SemaphoreType.DMA((n_slots,)),
        ],
    )
    return pl.pallas_call(
        functools.partial(_decode_attn_kernel, pages_per_dot),
        grid_spec=grid_spec,
        out_shape=jax.ShapeDtypeStruct((n_seq, rows, KV_LORA), F32),
        compiler_params=pltpu.CompilerParams(
            dimension_semantics=("arbitrary",), vmem_limit_bytes=VMEM_LIMIT),
        name="decode_attn",
    )(page_table, qcat, kcat.reshape(n_seq, t_new, QK_DIM), cache_ckv, cache_kpe_t)


def _uv_proj_kernel(o_ref, wuv_ref, a_ref):
    n_seq, _, t_new, _ = o_ref.shape
    o = o_ref[:, 0].reshape(n_seq * t_new, KV_LORA).astype(BF16)
    a_ref[...] = _dot(o, wuv_ref[0])


def _uv_proj(o_lat, wuv, *, t_new):
    n_seq = o_lat.shape[0]
    return pl.pallas_call(
        _uv_proj_kernel,
        grid=(N_HEADS,),
        in_specs=[
            pl.BlockSpec((n_seq, 1, t_new, KV_LORA), lambda h: (0, h, 0, 0)),
            pl.BlockSpec((1, KV_LORA, V_DIM), lambda h: (h, 0, 0)),
        ],
        out_specs=pl.BlockSpec((n_seq * t_new, V_DIM), lambda h: (0, h)),
        out_shape=jax.ShapeDtypeStruct((n_seq * t_new, D_ATTN), F32),
        compiler_params=pltpu.CompilerParams(dimension_semantics=("arbitrary",)),
        name="uv_proj",
    )(o_lat.reshape(n_seq, N_HEADS, t_new, KV_LORA), wuv)


def _out_mlp_kernel(x_ref, attn_ref, gate_ref, wo_ref, g2_ref, wup_ref, wdn_ref, fg_ref,
                    o_ref, hn_ref):
    f = pl.program_id(1)

    @pl.when(f == 0)
    def _():
        mix_o = (_dot(attn_ref[...].astype(BF16), wo_ref[0:D_ATTN, :])
                 + _dot(gate_ref[...], wo_ref[D_ATTN:, :]))
        h = x_ref[...] + mix_o
        o_ref[...] = h
        hn_ref[...] = _rms(h, g2_ref[...]).astype(BF16)

    a = _dot(hn_ref[...], wup_ref[...])
    a = jnp.square(jnp.maximum(a, 0.0)).astype(BF16)
    o_ref[...] += _dot(a, wdn_ref[...])

    @pl.when(f == pl.num_programs(1) - 1)
    def _():
        o_ref[...] = _rms(o_ref[...], fg_ref[...])


def _out_mlp(x, attn, gate, w, *, tm, tf):
    t = x.shape[0]
    row = lambda i, f: (i, 0)
    return pl.pallas_call(
        _out_mlp_kernel,
        grid=(t // tm, D_FF // tf),
        in_specs=[
            pl.BlockSpec((tm, D_MODEL), row),
            pl.BlockSpec((tm, D_ATTN), row),
            pl.BlockSpec((tm, D_CONV), row),
            _resident((D_MODEL, D_MODEL)),
            _resident((1, D_MODEL)),
            pl.BlockSpec((D_MODEL, tf), lambda i, f: (0, f)),
            pl.BlockSpec((tf, D_MODEL), lambda i, f: (f, 0)),
            _resident((1, D_MODEL)),
        ],
        out_specs=pl.BlockSpec((tm, D_MODEL), row),
        out_shape=jax.ShapeDtypeStruct((t, D_MODEL), F32),
        scratch_shapes=[pltpu.VMEM((tm, D_MODEL), BF16)],
        compiler_params=pltpu.CompilerParams(
            dimension_semantics=("arbitrary", "arbitrary"), vmem_limit_bytes=VMEM_LIMIT),
        name="out_mlp",
    )(x, attn, gate, w['wo'], w['g2'], w['wup'], w['wdn'], w['fg'])


def _with_rot_cols(a):
    half = ROPE_DIM // 2
    return jnp.concatenate([a, -a[..., half:], a[..., :half]], -1)


def _layer_weights(norm1_g, w_in, q_norm_g, w_uq, kv_norm_g, w_uk, w_uv, conv_w, norm2_g, final_g):
    o1 = Q_LORA
    o2 = o1 + KV_LORA
    o3 = o2 + ROPE_DIM
    w_in_t = w_in.T
    wkv_t = jnp.concatenate([w_in_t[o1:o2], _with_rot_cols(w_in_t[o2:o3].T).T], 0)
    wuq = w_uq.reshape(Q_LORA, N_HEADS, NOPE_DIM + ROPE_DIM)
    wuq = jnp.concatenate([wuq[..., :NOPE_DIM], _with_rot_cols(wuq[..., NOPE_DIM:])], -1)
    return {
        'g1': norm1_g.reshape(1, D_MODEL),
        'wq_t': w_in_t[:o1].astype(BF16),
        'wkv_t': wkv_t.astype(BF16),
        'wc_t': w_in_t[o3:].astype(BF16),
        'qg': q_norm_g.reshape(1, Q_LORA),
        'wuq': wuq.reshape(Q_LORA, N_HEADS * Q_COLS).astype(BF16),
        'kvg': kv_norm_g.reshape(1, KV_LORA),
        'wuk': w_uk.transpose(1, 2, 0).astype(BF16),
        'wuv': w_uv.transpose(1, 0, 2).astype(BF16),
        'wuv_t': w_uv.transpose(1, 2, 0).astype(BF16),
        'cw': conv_w,
        'g2': norm2_g.reshape(1, D_MODEL),
        'fg': final_g.reshape(1, D_MODEL),
    }


def _rope_table(pos):
    inv = ROPE_BASE ** (-jnp.arange(0, ROPE_DIM, 2, dtype=F32) / ROPE_DIM)
    ang = pos.astype(F32)[:, None] * inv[None, :]
    cos, sin = jnp.cos(ang), jnp.sin(ang)
    return jnp.concatenate([cos, cos, sin, sin], -1)


def kernel(x_prompt, x_sample, cache_ckv, cache_kpe, state_conv, page_table, norm1_g, w_in,
           q_norm_g, w_uq, kv_norm_g, w_uk, w_uv, conv_w, w_o, norm2_g, w_up, w_down, final_g):
    batch, seq, _ = x_prompt.shape
    dec_batch, t_new, _ = x_sample.shape
    depth = norm1_g.shape[0]
    assert depth == 1
    past = page_table.shape[1] * PAGE_SIZE
    tm_mix_p, tm_mix_s, tq, tm_mlp, tf = 512, 256, 256, 512, 1024

    w = _layer_weights(norm1_g[0], w_in[0], q_norm_g[0], w_uq[0], kv_norm_g[0], w_uk[0], w_uv[0],
                       conv_w[0], norm2_g[0], final_g)

    xp = x_prompt.reshape(batch * seq, D_MODEL)
    cs_p = _rope_table(jnp.arange(seq, dtype=jnp.int32))
    qt_p, k_p, ckv_p, kpe_p, gate_p, utail_p, kt_p = _mix_in(
        xp, w, cs_p, None, seq_len=seq, tm=tm_mix_p, tail=SUBLANES, tq=tq)
    attn_p, w['wo'], w['wup'], w['wdn'] = _prompt_attn(
        qt_p, kt_p, k_p, w['wuv_t'], w_o[0], w_up[0], w_down[0], batch=batch, seq=seq, tq=tq)
    y_p = _out_mlp(xp, attn_p, gate_p, w, tm=tm_mlp, tf=tf)
    conv_p = utail_p.reshape(batch, seq // tm_mix_p, SUBLANES, D_CONV)[:, -1, SUBLANES - (CONV_W - 1):]

    xs = x_sample.reshape(dec_batch * t_new, D_MODEL)
    cs_s = jnp.tile(_rope_table(past + jnp.arange(t_new, dtype=jnp.int32)), (tm_mix_s // t_new, 1))
    st = state_conv[0]
    zeros = jnp.zeros((dec_batch, t_new, D_CONV), F32)
    prev1 = zeros.at[:, 0].set(st[:, 1]).reshape(dec_batch * t_new, D_CONV)
    prev2 = zeros.at[:, 0].set(st[:, 0]).at[:, 1].set(st[:, 1]).reshape(dec_batch * t_new, D_CONV)
    q_s, k_s, ckv_s, kpe_s, gate_s, utail_s = _mix_in(
        xs, w, cs_s, (prev1, prev2), seq_len=t_new, tm=tm_mix_s, tail=tm_mix_s)
    kpe_t = jnp.swapaxes(cache_kpe.reshape(cache_kpe.shape[1:]), 1, 2)
    o_lat = _decode_attn(page_table, q_s, k_s, cache_ckv.reshape(cache_ckv.shape[1:]), kpe_t,
                         pages_per_chunk=16, pages_per_dot=8, n_slots=4)
    attn_s = _uv_proj(o_lat, w['wuv'], t_new=t_new)
    y_s = _out_mlp(xs, attn_s, gate_s, w, tm=tm_mlp, tf=tf)
    conv_s = utail_s.reshape(dec_batch, t_new, D_CONV)[:, t_new - (CONV_W - 1):]

    return (y_p.reshape(batch, seq, D_MODEL),
            y_s.reshape(dec_batch, t_new, D_MODEL),
            ckv_p.reshape(depth, batch, seq, KV_LORA),
            kpe_p.reshape(depth, batch, seq, ROPE_DIM),
            conv_p.reshape(depth, batch, CONV_W - 1, D_CONV),
            ckv_s.reshape(depth, dec_batch, t_new, KV_LORA),
            kpe_s.reshape(depth, dec_batch, t_new, ROPE_DIM),
            conv_s.reshape(depth, dec_batch, CONV_W - 1, D_CONV))
```

```python
import functools

import jax
import jax.numpy as jnp
from jax import lax
from jax.experimental import pallas as pl
from jax.experimental.pallas import tpu as pltpu

D_MODEL = 2048
N_HEADS = 8
Q_LORA = 512
KV_LORA = 512
NOPE_DIM = 128
ROPE_DIM = 64
V_DIM = 128
ROPE_BASE = 10000.0
D_ATTN = N_HEADS * V_DIM
D_CONV = D_MODEL - D_ATTN
CONV_W = 3
D_FF = 4 * D_MODEL
EPS = 1e-6
PAGE_SIZE = 128
SCALE = (NOPE_DIM + ROPE_DIM) ** -0.5

LANES = 128
SUBLANES = 8
ROPE_PAD = LANES
assert 2 * ROPE_DIM == ROPE_PAD
QK_DIM = KV_LORA + ROPE_PAD
Q_COLS = NOPE_DIM + ROPE_PAD
KV_COLS = KV_LORA + ROPE_PAD
VMEM_LIMIT = 58 * 1024 * 1024

F32 = jnp.float32
BF16 = jnp.bfloat16


def _dot(a, b):
    return jnp.dot(a, b, preferred_element_type=F32)


def _dot_t(a, b):
    return lax.dot_general(a, b, (((1,), (1,)), ((), ())), preferred_element_type=F32)


def _rms(x, g):
    r = lax.rsqrt(jnp.mean(x * x, axis=-1, keepdims=True) + EPS)
    return (x * r) * g


def _rope(xr, cs):
    t = xr * cs
    return t + pltpu.roll(t, ROPE_DIM, axis=1)


def _resident(shape):
    return pl.BlockSpec(shape, lambda *_: (0,) * len(shape), pipeline_mode=pl.Buffered(1))


def _store_transposed(dst_ref, latent, rope):
    n_tiles, _, tq = dst_ref.shape
    for t in range(n_tiles):
        rows = slice(t * tq, (t + 1) * tq)
        for j in range(KV_LORA // LANES):
            cols = slice(j * LANES, (j + 1) * LANES)
            dst_ref[t, cols, :] = latent[rows, cols].T.astype(BF16)
        dst_ref[t, KV_LORA:, :] = rope[rows].T.astype(BF16)


def _mix_kernel(seq_len, has_prev, conv_chunk, *refs):
    if has_prev:
        (x_ref, g1_ref, wq_ref, wkv_ref, wc_ref, qg_ref, wuq_ref, kvg_ref, wuk_ref, cw_ref,
         cs_ref, p1_ref, p2_ref,
         qcat_ref, kcat_ref, ckv_ref, kpe_ref, gate_ref, utail_ref, ubuf_ref) = refs
    else:
        (x_ref, g1_ref, wq_ref, wkv_ref, wc_ref, qg_ref, wuq_ref, kvg_ref, wuk_ref, cw_ref,
         cs_ref,
         qcat_ref, kcat_ref, ckv_ref, kpe_ref, gate_ref, utail_ref, kt_ref, ubuf_ref) = refs
    tm = x_ref.shape[0]
    tail = utail_ref.shape[-2]
    i = pl.program_id(0)

    @pl.when(i == 0)
    def _():
        ubuf_ref[0:SUBLANES, :] = jnp.zeros((SUBLANES, D_CONV), F32)

    xn = _rms(x_ref[...], g1_ref[...]).astype(BF16)
    cs = cs_ref[...]

    zkv = _dot_t(xn, wkv_ref[...])
    ckv = _rms(zkv[:, :KV_LORA], kvg_ref[...])
    lane = lax.broadcasted_iota(jnp.int32, (tm, ROPE_PAD), 1)
    kpe = jnp.where(lane < ROPE_DIM, _rope(zkv[:, KV_LORA:], cs), 0.0)
    ckv_ref[...] = ckv
    kpe_ref[...] = kpe[:, :ROPE_DIM]
    kcat_ref[:, :KV_LORA] = ckv.astype(BF16)
    kcat_ref[:, KV_LORA:] = kpe.astype(BF16)
    if not has_prev:
        _store_transposed(kt_ref, ckv, kpe)

    rq = _rms(_dot_t(xn, wq_ref[...]), qg_ref[...]).astype(BF16)
    for h in range(N_HEADS):
        qh = _dot(rq, wuq_ref[:, h * Q_COLS:(h + 1) * Q_COLS])
        q_abs = _dot(qh[:, :NOPE_DIM].astype(BF16), wuk_ref[h]) * SCALE
        q_pe = _rope(qh[:, NOPE_DIM:], cs) * SCALE
        if has_prev:
            nq, _, tq, _ = qcat_ref.shape
            qcat_ref[:, h, :, 0:KV_LORA] = q_abs.reshape(nq, tq, KV_LORA)
            qcat_ref[:, h, :, KV_LORA:] = q_pe.reshape(nq, tq, ROPE_PAD)
        else:
            _store_transposed(qcat_ref.at[:, h], q_abs, q_pe)

    pos = (i * tm + lax.broadcasted_iota(jnp.int32, (tm, 1), 0)) % seq_len
    m1 = pos >= 1
    m2 = pos >= 2
    for j in range(D_CONV // conv_chunk):
        sl = slice(j * conv_chunk, (j + 1) * conv_chunk)
        b_g = _dot_t(xn, wc_ref[sl, :])
        c_g = _dot_t(xn, wc_ref[D_CONV + j * conv_chunk:D_CONV + (j + 1) * conv_chunk, :])
        x_in = _dot_t(xn, wc_ref[2 * D_CONV + j * conv_chunk:2 * D_CONV + (j + 1) * conv_chunk, :])
        u = c_g * x_in
        ubuf_ref[SUBLANES:SUBLANES + tm, sl] = u
        u1 = ubuf_ref[SUBLANES - 1:SUBLANES - 1 + tm, sl]
        u2 = ubuf_ref[SUBLANES - 2:SUBLANES - 2 + tm, sl]
        if has_prev:
            u1 = jnp.where(m1, u1, p1_ref[:, sl])
            u2 = jnp.where(m2, u2, p2_ref[:, sl])
        else:
            u1 = jnp.where(m1, u1, 0.0)
            u2 = jnp.where(m2, u2, 0.0)
        y = cw_ref[0:1, sl] * u2 + cw_ref[1:2, sl] * u1 + cw_ref[2:3, sl] * u
        gate_ref[:, sl] = (b_g * y).astype(BF16)
    utail_ref[0] = ubuf_ref[SUBLANES + tm - tail:SUBLANES + tm, :]
    ubuf_ref[0:SUBLANES, :] = ubuf_ref[tm:tm + SUBLANES, :]


def _mix_in(x, w, cs, prev, *, seq_len, tm, tail, tq=None):
    t = x.shape[0]
    nt = t // tm
    has_prev = prev is not None
    tab_blocks = cs.shape[0] // tm
    row = lambda i: (i, 0)
    in_specs = [
        pl.BlockSpec((tm, D_MODEL), row),
        _resident((1, D_MODEL)),
        _resident((Q_LORA, D_MODEL)),
        _resident((KV_COLS, D_MODEL)),
        _resident((3 * D_CONV, D_MODEL)),
        _resident((1, Q_LORA)),
        _resident((Q_LORA, N_HEADS * Q_COLS)),
        _resident((1, KV_LORA)),
        _resident((N_HEADS, NOPE_DIM, KV_LORA)),
        _resident((CONV_W, D_CONV)),
        pl.BlockSpec((tm, ROPE_PAD), lambda i: (i % tab_blocks, 0)),
    ]
    args = [x, w['g1'], w['wq_t'], w['wkv_t'], w['wc_t'], w['qg'], w['wuq'], w['kvg'], w['wuk'], w['cw'],
            cs]
    if has_prev:
        in_specs += [pl.BlockSpec((tm, D_CONV), row), pl.BlockSpec((tm, D_CONV), row)]
        args += list(prev)
    if has_prev:
        q_shape = jax.ShapeDtypeStruct((t // seq_len, N_HEADS, seq_len, QK_DIM), F32)
        q_spec = pl.BlockSpec((tm // seq_len, N_HEADS, seq_len, QK_DIM), lambda i: (i, 0, 0, 0))
    else:
        q_shape = jax.ShapeDtypeStruct((t // tq, N_HEADS, QK_DIM, tq), BF16)
        q_spec = pl.BlockSpec((tm // tq, N_HEADS, QK_DIM, tq), lambda i: (i, 0, 0, 0))
    out_shape = [
        q_shape,
        jax.ShapeDtypeStruct((t, QK_DIM), BF16),
        jax.ShapeDtypeStruct((t, KV_LORA), F32),
        jax.ShapeDtypeStruct((t, ROPE_DIM), F32),
        jax.ShapeDtypeStruct((t, D_CONV), BF16),
        jax.ShapeDtypeStruct((nt, tail, D_CONV), F32),
    ]
    out_specs = [
        q_spec,
        pl.BlockSpec((tm, QK_DIM), row),
        pl.BlockSpec((tm, KV_LORA), row),
        pl.BlockSpec((tm, ROPE_DIM), row),
        pl.BlockSpec((tm, D_CONV), row),
        pl.BlockSpec((1, tail, D_CONV), lambda i: (i, 0, 0)),
    ]
    if not has_prev:
        out_shape.append(jax.ShapeDtypeStruct((t // tq, QK_DIM, tq), BF16))
        out_specs.append(pl.BlockSpec((tm // tq, QK_DIM, tq), lambda i: (i, 0, 0)))
    return pl.pallas_call(
        functools.partial(_mix_kernel, seq_len, has_prev, 256),
        grid=(nt,),
        in_specs=in_specs,
        out_specs=out_specs,
        out_shape=out_shape,
        scratch_shapes=[pltpu.VMEM((tm + SUBLANES, D_CONV), F32)],
        compiler_params=pltpu.CompilerParams(
            dimension_semantics=("arbitrary",), vmem_limit_bytes=VMEM_LIMIT),
        name="mix_in_prev" if has_prev else "mix_in",
    )(*args)


def _prompt_attn_kernel(qt_ref, kt_ref, k_ref, wuvt_ref, wo_f32, wup_f32, wdn_f32,
                        o_ref, wo_bf16, wup_bf16, wdn_bf16, m_ref, l_ref, acc_ref):
    tq = qt_ref.shape[-1]
    tk = k_ref.shape[1]
    qi = pl.program_id(1)

    wo_bf16[...] = wo_f32[...].astype(BF16)
    wup_bf16[...] = wup_f32[...].astype(BF16)
    wdn_bf16[...] = wdn_f32[...].astype(BF16)

    def all_scores(k):
        return [_dot(k, qt_ref[0, h]) for h in range(N_HEADS)]

    k = k_ref[qi]
    v_t = kt_ref[qi, 0:KV_LORA, :]
    visible = (lax.broadcasted_iota(jnp.int32, (tk, tq), 0)
               <= lax.broadcasted_iota(jnp.int32, (tk, tq), 1))
    for h, s_t in enumerate(all_scores(k)):
        s_t = jnp.where(visible, s_t, -jnp.inf)
        m = jnp.max(s_t, axis=0, keepdims=True)
        p_t = jnp.exp(s_t - m)
        m_ref[h] = m
        l_ref[h] = jnp.sum(p_t, axis=0, keepdims=True)
        acc_ref[h] = _dot(v_t, p_t.astype(BF16))

    def update(k, v_t):
        for h, s_t in enumerate(all_scores(k)):
            m_old = m_ref[h]
            m_new = jnp.maximum(m_old, jnp.max(s_t, axis=0, keepdims=True))
            corr = jnp.exp(m_old - m_new)
            p_t = jnp.exp(s_t - m_new)
            l_ref[h] = l_ref[h] * corr + jnp.sum(p_t, axis=0, keepdims=True)
            acc_ref[h] = acc_ref[h] * corr + _dot(v_t, p_t.astype(BF16))
            m_ref[h] = m_new

    def pair_step(i, carry):
        k2 = k_ref[pl.ds(2 * i, 2)].reshape(2 * tk, QK_DIM)
        v_t2 = jnp.concatenate([kt_ref[2 * i, 0:KV_LORA, :], kt_ref[2 * i + 1, 0:KV_LORA, :]], axis=1)
        update(k2, v_t2)
        return carry

    lax.fori_loop(0, qi // 2, pair_step, 0)

    @pl.when(qi % 2 == 1)
    def _():
        update(k_ref[qi - 1], kt_ref[qi - 1, 0:KV_LORA, :])

    for h in range(N_HEADS):
        o_t = (acc_ref[h] * (1.0 / l_ref[h])).astype(BF16)
        a_t = _dot(wuvt_ref[h], o_t)
        o_ref[:, h * V_DIM:(h + 1) * V_DIM] = a_t.T.astype(o_ref.dtype)


def _prompt_attn(q_t, kcat_t, kcat, wuv_t, w_o, w_up, w_down, *, batch, seq, tq):
    nq = seq // tq
    steps = batch * nq
    step_rows = lambda b, i: (b * nq + i, 0)
    weights = (w_o, w_up, w_down)
    slabs = [(wt.shape[0] // steps, wt.shape[1]) for wt in weights]
    return pl.pallas_call(
        _prompt_attn_kernel,
        grid=(batch, nq),
        in_specs=[
            pl.BlockSpec((1, N_HEADS, QK_DIM, tq), lambda b, i: (b * nq + i, 0, 0, 0)),
            pl.BlockSpec((nq, QK_DIM, tq), lambda b, i: (b, 0, 0)),
            pl.BlockSpec((nq, tq, QK_DIM), lambda b, i: (b, 0, 0)),
            _resident((N_HEADS, V_DIM, KV_LORA)),
        ] + [pl.BlockSpec(slab, step_rows) for slab in slabs],
        out_specs=[pl.BlockSpec((tq, D_ATTN), step_rows)] + [pl.BlockSpec(slab, step_rows) for slab in slabs],
        out_shape=[jax.ShapeDtypeStruct((batch * seq, D_ATTN), BF16)]
        + [jax.ShapeDtypeStruct(wt.shape, BF16) for wt in weights],
        scratch_shapes=[pltpu.VMEM((N_HEADS, 1, tq), F32), pltpu.VMEM((N_HEADS, 1, tq), F32),
                        pltpu.VMEM((N_HEADS, KV_LORA, tq), F32)],
        compiler_params=pltpu.CompilerParams(
            dimension_semantics=("arbitrary", "arbitrary"), vmem_limit_bytes=VMEM_LIMIT),
        name="prompt_attn",
    )(q_t, kcat_t, kcat.reshape(batch * nq, tq, QK_DIM), wuv_t, *weights)


def _decode_attn_kernel(pages_per_dot, pt_ref, q_ref, kn_ref, ckv_hbm, kpet_hbm,
                        o_ref, kbuf, pbuf, sem):
    n_seq, n_pages = pt_ref.shape
    n_slots, pages_per_chunk = kbuf.shape[0], kbuf.shape[1]
    n_chunks = n_pages // pages_per_chunk
    ahead = n_slots - 1
    dot_keys = pages_per_dot * PAGE_SIZE
    t_new = kn_ref.shape[1]
    rows = N_HEADS * t_new
    b = pl.program_id(0)

    def copies(g):
        seq, chunk, slot = g // n_chunks, g % n_chunks, g % n_slots
        out = []
        for p in range(pages_per_chunk):
            page = pt_ref[seq, chunk * pages_per_chunk + p]
            out.append(pltpu.make_async_copy(ckv_hbm.at[page], kbuf.at[slot, p], sem.at[slot]))
            out.append(pltpu.make_async_copy(kpet_hbm.at[page], pbuf.at[slot, p], sem.at[slot]))
        return out

    @pl.when(b == 0)
    def _():
        for g in range(ahead):
            for c in copies(g):
                c.start()

    q = q_ref[0].reshape(rows, QK_DIM).astype(BF16)
    qa = q[:, :KV_LORA]
    qp = q[:, KV_LORA:KV_LORA + ROPE_DIM]

    kn = kn_ref[0]
    s = _dot_t(q, kn)
    t_row = lax.broadcasted_iota(jnp.int32, (rows, t_new), 0) % t_new
    t_col = lax.broadcasted_iota(jnp.int32, (rows, t_new), 1)
    s = jnp.where(t_col <= t_row, s, -jnp.inf)
    m0 = jnp.max(s, axis=-1, keepdims=True)
    p0 = jnp.exp(s - m0)
    init = (m0, jnp.sum(p0, axis=-1, keepdims=True), _dot(p0.astype(BF16), kn[:, :KV_LORA]))

    def chunk_step(c, carry):
        m, l, acc = carry
        g = b * n_chunks + c
        slot = g % n_slots

        @pl.when(g + ahead < n_seq * n_chunks)
        def _():
            for cp in copies(g + ahead):
                cp.start()

        for cp in copies(g):
            cp.wait()
        keys, scores = [], []
        for j in range(pages_per_chunk // pages_per_dot):
            pages = range(j * pages_per_dot, (j + 1) * pages_per_dot)
            k = kbuf[slot, j * pages_per_dot:(j + 1) * pages_per_dot].reshape(dot_keys, KV_LORA)
            k = k.astype(BF16)
            kp_t = jnp.concatenate([pbuf[slot, p] for p in pages], axis=1).astype(BF16)
            keys.append(k)
            scores.append(_dot_t(qa, k) + _dot(qp, kp_t))
        for k, s in zip(keys, scores):
            m_new = jnp.maximum(m, jnp.max(s, axis=-1, keepdims=True))
            corr = jnp.exp(m - m_new)
            p = jnp.exp(s - m_new)
            l = l * corr + jnp.sum(p, axis=-1, keepdims=True)
            acc = acc * corr + _dot(p.astype(BF16), k)
            m = m_new
        return m, l, acc

    _, l, acc = lax.fori_loop(0, n_chunks, chunk_step, init)
    o_ref[0] = acc * (1.0 / l)


def _decode_attn(page_table, qcat, kcat, cache_ckv, cache_kpe_t, *, pages_per_chunk, pages_per_dot,
                 n_slots):
    n_seq, n_pages = page_table.shape
    t_new = qcat.shape[2]
    rows = N_HEADS * t_new
    assert n_pages % pages_per_chunk == 0 and pages_per_chunk % pages_per_dot == 0
    assert n_slots - 1 <= n_seq * (n_pages // pages_per_chunk)
    grid_spec = pltpu.PrefetchScalarGridSpec(
        num_scalar_prefetch=1,
        grid=(n_seq,),
        in_specs=[
            pl.BlockSpec((1, N_HEADS, t_new, QK_DIM), lambda b, pt: (b, 0, 0, 0)),
            pl.BlockSpec((1, t_new, QK_DIM), lambda b, pt: (b, 0, 0)),
            pl.BlockSpec(memory_space=pl.ANY),
            pl.BlockSpec(memory_space=pl.ANY),
        ],
        out_specs=pl.BlockSpec((1, rows, KV_LORA), lambda b, pt: (b, 0, 0)),
        scratch_shapes=[
            pltpu.VMEM((n_slots, pages_per_chunk, PAGE_SIZE, KV_LORA), F32),
            pltpu.VMEM((n_slots, pages_per_chunk, ROPE_DIM, PAGE_SIZE), F32),
            pltpu.SemaphoreType.DMA((n_slots,)),
        ],
    )
    return pl.pallas_call(
        functools.partial(_decode_attn_kernel, pages_per_dot),
        grid_spec=grid_spec,
        out_shape=jax.ShapeDtypeStruct((n_seq, rows, KV_LORA), F32),
        compiler_params=pltpu.CompilerParams(
            dimension_semantics=("arbitrary",), vmem_limit_bytes=VMEM_LIMIT),
        name="decode_attn",
    )(page_table, qcat, kcat.reshape(n_seq, t_new, QK_DIM), cache_ckv, cache_kpe_t)


def _uv_proj_kernel(o_ref, wuv_ref, a_ref):
    n_seq, _, t_new, _ = o_ref.shape
    o = o_ref[:, 0].reshape(n_seq * t_new, KV_LORA).astype(BF16)
    a_ref[...] = _dot(o, wuv_ref[0])


def _uv_proj(o_lat, wuv, *, t_new):
    n_seq = o_lat.shape[0]
    return pl.pallas_call(
        _uv_proj_kernel,
        grid=(N_HEADS,),
        in_specs=[
            pl.BlockSpec((n_seq, 1, t_new, KV_LORA), lambda h: (0, h, 0, 0)),
            pl.BlockSpec((1, KV_LORA, V_DIM), lambda h: (h, 0, 0)),
        ],
        out_specs=pl.BlockSpec((n_seq * t_new, V_DIM), lambda h: (0, h)),
        out_shape=jax.ShapeDtypeStruct((n_seq * t_new, D_ATTN), F32),
        compiler_params=pltpu.CompilerParams(dimension_semantics=("arbitrary",)),
        name="uv_proj",
    )(o_lat.reshape(n_seq, N_HEADS, t_new, KV_LORA), wuv)


def _out_mlp_kernel(x_ref, attn_ref, gate_ref, wo_ref, g2_ref, wup_ref, wdn_ref, fg_ref,
                    o_ref, hn_ref):
    f = pl.program_id(1)

    @pl.when(f == 0)
    def _():
        mix_o = (_dot(attn_ref[...].astype(BF16), wo_ref[0:D_ATTN, :])
                 + _dot(gate_ref[...], wo_ref[D_ATTN:, :]))
        h = x_ref[...] + mix_o
        o_ref[...] = h
        hn_ref[...] = _rms(h, g2_ref[...]).astype(BF16)

    a = _dot(hn_ref[...], wup_ref[...])
    a = jnp.square(jnp.maximum(a, 0.0)).astype(BF16)
    o_ref[...] += _dot(a, wdn_ref[...])

    @pl.when(f == pl.num_programs(1) - 1)
    def _():
        o_ref[...] = _rms(o_ref[...], fg_ref[...])


def _out_mlp(x, attn, gate, w, *, tm, tf):
    t = x.shape[0]
    row = lambda i, f: (i, 0)
    return pl.pallas_call(
        _out_mlp_kernel,
        grid=(t // tm, D_FF // tf),
        in_specs=[
            pl.BlockSpec((tm, D_MODEL), row),
            pl.BlockSpec((tm, D_ATTN), row),
            pl.BlockSpec((tm, D_CONV), row),
            _resident((D_MODEL, D_MODEL)),
            _resident((1, D_MODEL)),
            pl.BlockSpec((D_MODEL, tf), lambda i, f: (0, f)),
            pl.BlockSpec((tf, D_MODEL), lambda i, f: (f, 0)),
            _resident((1, D_MODEL)),
        ],
        out_specs=pl.BlockSpec((tm, D_MODEL), row),
        out_shape=jax.ShapeDtypeStruct((t, D_MODEL), F32),
        scratch_shapes=[pltpu.VMEM((tm, D_MODEL), BF16)],
        compiler_params=pltpu.CompilerParams(
            dimension_semantics=("arbitrary", "arbitrary"), vmem_limit_bytes=VMEM_LIMIT),
        name="out_mlp",
    )(x, attn, gate, w['wo'], w['g2'], w['wup'], w['wdn'], w['fg'])


def _with_rot_cols(a):
    half = ROPE_DIM // 2
    return jnp.concatenate([a, -a[..., half:], a[..., :half]], -1)


def _layer_weights(norm1_g, w_in, q_norm_g, w_uq, kv_norm_g, w_uk, w_uv, conv_w, norm2_g, final_g):
    o1 = Q_LORA
    o2 = o1 + KV_LORA
    o3 = o2 + ROPE_DIM
    w_in_t = w_in.T
    wkv_t = jnp.concatenate([w_in_t[o1:o2], _with_rot_cols(w_in_t[o2:o3].T).T], 0)
    wuq = w_uq.reshape(Q_LORA, N_HEADS, NOPE_DIM + ROPE_DIM)
    wuq = jnp.concatenate([wuq[..., :NOPE_DIM], _with_rot_cols(wuq[..., NOPE_DIM:])], -1)
    return {
        'g1': norm1_g.reshape(1, D_MODEL),
        'wq_t': w_in_t[:o1].astype(BF16),
        'wkv_t': wkv_t.astype(BF16),
        'wc_t': w_in_t[o3:].astype(BF16),
        'qg': q_norm_g.reshape(1, Q_LORA),
        'wuq': wuq.reshape(Q_LORA, N_HEADS * Q_COLS).astype(BF16),
        'kvg': kv_norm_g.reshape(1, KV_LORA),
        'wuk': w_uk.transpose(1, 2, 0).astype(BF16),
        'wuv': w_uv.transpose(1, 0, 2).astype(BF16),
        'wuv_t': w_uv.transpose(1, 2, 0).astype(BF16),
        'cw': conv_w,
        'g2': norm2_g.reshape(1, D_MODEL),
        'fg': final_g.reshape(1, D_MODEL),
    }


def _rope_table(pos):
    inv = ROPE_BASE ** (-jnp.arange(0, ROPE_DIM, 2, dtype=F32) / ROPE_DIM)
    ang = pos.astype(F32)[:, None] * inv[None, :]
    cos, sin = jnp.cos(ang), jnp.sin(ang)
    return jnp.concatenate([cos, cos, sin, sin], -1)


def kernel(x_prompt, x_sample, cache_ckv, cache_kpe, state_conv, page_table, norm1_g, w_in,
           q_norm_g, w_uq, kv_norm_g, w_uk, w_uv, conv_w, w_o, norm2_g, w_up, w_down, final_g):
    batch, seq, _ = x_prompt.shape
    dec_batch, t_new, _ = x_sample.shape
    depth = norm1_g.shape[0]
    assert depth == 1
    past = page_table.shape[1] * PAGE_SIZE
    tm_mix_p, tm_mix_s, tq, tm_mlp, tf = 512, 256, 256, 512, 1024

    w = _layer_weights(norm1_g[0], w_in[0], q_norm_g[0], w_uq[0], kv_norm_g[0], w_uk[0], w_uv[0],
                       conv_w[0], norm2_g[0], final_g)

    xp = x_prompt.reshape(batch * seq, D_MODEL)
    cs_p = _rope_table(jnp.arange(seq, dtype=jnp.int32))
    qt_p, k_p, ckv_p, kpe_p, gate_p, utail_p, kt_p = _mix_in(
        xp, w, cs_p, None, seq_len=seq, tm=tm_mix_p, tail=SUBLANES, tq=tq)
    attn_p, w['wo'], w['wup'], w['wdn'] = _prompt_attn(
        qt_p, kt_p, k_p, w['wuv_t'], w_o[0], w_up[0], w_down[0], batch=batch, seq=seq, tq=tq)
    y_p = _out_mlp(xp, attn_p, gate_p, w, tm=tm_mlp, tf=tf)
    conv_p = utail_p.reshape(batch, seq // tm_mix_p, SUBLANES, D_CONV)[:, -1, SUBLANES - (CONV_W - 1):]

    xs = x_sample.reshape(dec_batch * t_new, D_MODEL)
    cs_s = jnp.tile(_rope_table(past + jnp.arange(t_new, dtype=jnp.int32)), (tm_mix_s // t_new, 1))
    st = state_conv[0]
    zeros = jnp.zeros((dec_batch, t_new, D_CONV), F32)
    prev1 = zeros.at[:, 0].set(st[:, 1]).reshape(dec_batch * t_new, D_CONV)
    prev2 = zeros.at[:, 0].set(st[:, 0]).at[:, 1].set(st[:, 1]).reshape(dec_batch * t_new, D_CONV)
    q_s, k_s, ckv_s, kpe_s, gate_s, utail_s = _mix_in(
        xs, w, cs_s, (prev1, prev2), seq_len=t_new, tm=tm_mix_s, tail=tm_mix_s)
    kpe_t = jnp.swapaxes(cache_kpe.reshape(cache_kpe.shape[1:]), 1, 2)
    o_lat = _decode_attn(page_table, q_s, k_s, cache_ckv.reshape(cache_ckv.shape[1:]), kpe_t,
                         pages_per_chunk=32, pages_per_dot=8, n_slots=3)
    attn_s = _uv_proj(o_lat, w['wuv'], t_new=t_new)
    y_s = _out_mlp(xs, attn_s, gate_s, w, tm=tm_mlp, tf=tf)
    conv_s = utail_s.reshape(dec_batch, t_new, D_CONV)[:, t_new - (CONV_W - 1):]

    return (y_p.reshape(batch, seq, D_MODEL),
            y_s.reshape(dec_batch, t_new, D_MODEL),
            ckv_p.reshape(depth, batch, seq, KV_LORA),
            kpe_p.reshape(depth, batch, seq, ROPE_DIM),
            conv_p.reshape(depth, batch, CONV_W - 1, D_CONV),
            ckv_s.reshape(depth, dec_batch, t_new, KV_LORA),
            kpe_s.reshape(depth, dec_batch, t_new, ROPE_DIM),
            conv_s.reshape(depth, dec_batch, CONV_W - 1, D_CONV))
```
